```python
import math
import jax
import jax.numpy as jnp
from jax import lax
import numpy as np

D_MODEL = 2048
BATCH = 2
SEQ = 4096
DEPTH = 4
DEC_BATCH = 32
DEC_SEQ = 8
PAST_LEN = 16384
PAGE_SIZE = 128

HEAD_DIM = 64
BRANCH_W = 3 * D_MODEL // 8
N_BRANCH = 5
MIX_W = N_BRANCH * BRANCH_W
A_HEADS = BRANCH_W // HEAD_DIM
A_KV_HEADS = 4
A_WINDOW = 128
B_CH = BRANCH_W
B_CONV = 31
C_CH = BRANCH_W
C_WINDOWS = (2, 4, 8, 16)
C_GROUP = C_CH // 4
C_STATE = 16 - 1
D_HEADS = BRANCH_W // HEAD_DIM
D_PATTERNS = ((128, 1), (512, 4), (2048, 16))
D_WINDOW = 2048
M_HEADS = 4
M_HEAD_DIM = BRANCH_W // M_HEADS
N_MEM = 256
ROT_DIM = HEAD_DIM // 4
ROPE_THETA = 500000.0
BLOCK = 128
EPS = 1e-6
NEG = -1e30
IN_SIZES = (A_HEADS * HEAD_DIM, A_KV_HEADS * HEAD_DIM, A_KV_HEADS * HEAD_DIM, 2 * B_CH, C_CH,
            D_HEADS * HEAD_DIM, D_HEADS * HEAD_DIM, D_HEADS * HEAD_DIM, M_HEADS * M_HEAD_DIM, MIX_W)
IN_W = sum(IN_SIZES)

kernel_name = 'hybrid_parallel_groups_decode_step'


def _in_offsets():
    return np.cumsum(np.array(IN_SIZES))[:-1].tolist()


def rmsnorm(x, g):
    xf = x.astype(jnp.float32)
    y = xf * lax.rsqrt(jnp.mean(xf * xf, axis=-1, keepdims=True) + EPS)
    return (y * g.astype(jnp.float32)).astype(x.dtype)


def rope(x, pos):
    half = ROT_DIM // 2
    inv = ROPE_THETA ** (-jnp.arange(half, dtype=jnp.float32) / half)
    ang = pos.astype(jnp.float32)[:, None] * inv[None, :]
    cos = jnp.cos(ang)[None, :, None, :]
    sin = jnp.sin(ang)[None, :, None, :]
    xf = x.astype(jnp.float32)
    x1, x2, rest = xf[..., :half], xf[..., half:ROT_DIM], xf[..., ROT_DIM:]
    out = jnp.concatenate([x1 * cos - x2 * sin, x2 * cos + x1 * sin, rest], axis=-1)
    return out.astype(x.dtype)


def masked_attn(q, k, v, qpos, kpos, window, sink):
    scale = q.shape[-1] ** -0.5
    s = jnp.einsum('bnqhgd,bnkhd->bnqhgk', q.astype(jnp.float32), k.astype(jnp.float32)) * scale
    dist = qpos[:, :, None] - kpos[:, None, :]
    valid = (dist >= 0) & (dist <= window) & (kpos[:, None, :] >= 0)
    s = jnp.where(valid[None, :, :, None, None, :], s, NEG)
    m = jnp.max(s, axis=-1, keepdims=True)
    if sink is not None:
        sk = sink.astype(jnp.float32)[None, None, None, :, :, None]
        m = jnp.maximum(m, sk)
    p = jnp.exp(s - m)
    den = jnp.sum(p, axis=-1, keepdims=True)
    if sink is not None:
        den = den + jnp.exp(sk - m)
    out = jnp.einsum('bnqhgk,bnkhd->bnqhgd', p, v.astype(jnp.float32)) / den
    return out, (m + jnp.log(den))[..., 0]


def band_attn(q, k_all, v_all, qpos, kpos_all, window, sink):
    B, n = q.shape[0], q.shape[1]
    blk = math.gcd(n, BLOCK)
    nb = n // blk
    idx = jnp.arange(nb)[:, None] * blk + jnp.arange(blk + window)[None, :]
    qb = q.reshape((B, nb, blk) + q.shape[2:])
    out, lse = masked_attn(qb, k_all[:, idx], v_all[:, idx], qpos.reshape(nb, blk), kpos_all[idx],
                           window, sink)
    return out.reshape(q.shape), lse.reshape(q.shape[:-1])


def sliding_attn(q, k, v, pos, buf_kv, sink):
    B, T = q.shape[0], q.shape[1]
    G = A_HEADS // A_KV_HEADS
    kv_all = jnp.concatenate([buf_kv, jnp.stack([k, v], axis=2).astype(buf_kv.dtype)], axis=1)
    kpos = jnp.concatenate([pos[0] - A_WINDOW + jnp.arange(A_WINDOW), pos])
    o, _ = band_attn(q.reshape(B, T, A_KV_HEADS, G, HEAD_DIM), kv_all[:, :, 0], kv_all[:, :, 1],
                     pos, kpos, A_WINDOW, sink.reshape(A_KV_HEADS, G))
    return o.reshape(B, T, A_HEADS * HEAD_DIM), kv_all[:, T:]


def combine_by_denominator(outs, lses):
    wts = jax.nn.softmax(jnp.stack(lses, axis=0), axis=0)
    return jnp.sum(wts[..., None] * jnp.stack(outs, axis=0), axis=0)


def dilated_prompt(q, k, v):
    B, S, H, hd = q.shape
    outs, lses = [], []
    for w, d in D_PATTERNS:
        n, wd = S // d, w // d

        def to_res(x):
            return x.reshape(B, n, d, H, hd).swapaxes(1, 2).reshape(B * d, n, H, hd)

        pad = jnp.zeros((B * d, wd, H, hd), k.dtype)
        ks = jnp.concatenate([pad, to_res(k)], axis=1)
        vs = jnp.concatenate([pad, to_res(v)], axis=1)
        o, l = band_attn(to_res(q)[:, :, :, None], ks, vs, jnp.arange(n), jnp.arange(-wd, n), wd, None)
        outs.append(o.reshape(B, d, n, H, hd).swapaxes(1, 2).reshape(B, S, H, hd))
        lses.append(l.reshape(B, d, n, H).swapaxes(1, 2).reshape(B, S, H))
    y = combine_by_denominator(outs, lses).reshape(B, S, H * hd)
    L = min(D_WINDOW, S)
    return y, jnp.stack([k, v], axis=2)[:, S - L:]


def dilated_sample(q, k, v, pos, buf_kv):
    B, T, H, hd = q.shape
    L = buf_kv.shape[1]
    kv_all = jnp.concatenate([buf_kv, jnp.stack([k, v], axis=2).astype(buf_kv.dtype)], axis=1)
    outs, lses = [], []
    for w, d in D_PATTERNS:
        steps = jnp.arange(w // d + 1) * d
        idx = L + jnp.arange(T)[:, None] - steps[None, :]
        kpos = jnp.where(idx >= 0, pos[:, None] - steps[None, :], -1)
        g = kv_all[:, jnp.maximum(idx, 0)]
        o, l = masked_attn(q[:, :, None, :, None, :], g[:, :, :, 0], g[:, :, :, 1],
                           pos[:, None], kpos, w, None)
        outs.append(o.reshape(B, T, H, hd))
        lses.append(l.reshape(B, T, H))
    y = combine_by_denominator(outs, lses).reshape(B, T, H * hd)
    return y, kv_all[:, T:]


def conformer_conv(z, buf, conv_w, conv_b, ln_g, ln_b, w_pw):
    a, b = jnp.split(z, 2, axis=-1)
    g = a * jax.nn.sigmoid(b)
    gcat = jnp.concatenate([buf, g.astype(buf.dtype)], axis=1)
    y = lax.conv_general_dilated(gcat, conv_w[:, None, :].astype(gcat.dtype), window_strides=(1,),
                                 padding='VALID', dimension_numbers=('NWC', 'WIO', 'NWC'),
                                 feature_group_count=gcat.shape[-1])
    yf = y.astype(jnp.float32) + conv_b.astype(jnp.float32)
    mu = jnp.mean(yf, axis=-1, keepdims=True)
    var = jnp.mean(jnp.square(yf - mu), axis=-1, keepdims=True)
    yn = (yf - mu) * lax.rsqrt(var + EPS) * ln_g.astype(jnp.float32) + ln_b.astype(jnp.float32)
    out = jax.nn.silu(yn).astype(z.dtype) @ w_pw
    return out, gcat[:, gcat.shape[1] - (B_CONV - 1):]


def pool_mixer(u, buf, pos, w_grp, scale):
    B, T, C = u.shape
    ucat = jnp.concatenate([buf, u.astype(buf.dtype)], axis=1)
    cs = jnp.cumsum(ucat.astype(jnp.float32), axis=1)
    cs = jnp.concatenate([jnp.zeros((B, 1, C), jnp.float32), cs], axis=1)
    end = cs[:, C_STATE + 1:]
    means = []
    for gi, w in enumerate(C_WINDOWS):
        cols = slice(gi * C_GROUP, (gi + 1) * C_GROUP)
        start = cs[:, C_STATE + 1 - w: C_STATE + 1 - w + T, cols]
        cnt = jnp.minimum(pos + 1, w).astype(jnp.float32)[None, :, None]
        means.append((end[:, :, cols] - start) / cnt)
    dlt = (jnp.concatenate(means, axis=-1) - u.astype(jnp.float32)).reshape(B, T, len(C_WINDOWS), C_GROUP)
    y = jnp.einsum('btgc,gce->btge', dlt, w_grp.astype(jnp.float32)).reshape(B, T, C)
    return y * scale.astype(jnp.float32), ucat[:, T:]


def mem_kv_proj(mem, g, w_kv, k_norm):
    B, M = mem.shape[0], mem.shape[1]
    kv = (rmsnorm(mem, g) @ w_kv).reshape(B, M, 2, M_HEADS, M_HEAD_DIM)
    return jnp.stack([rmsnorm(kv[:, :, 0], k_norm), kv[:, :, 1]], axis=2)


def mem_attn(q, mkv, q_norm):
    B, T = q.shape[0], q.shape[1]
    qf = rmsnorm(q, q_norm).astype(jnp.float32) * M_HEAD_DIM ** -0.5
    s = jnp.einsum('bthd,bmhd->bhtm', qf, mkv[:, :, 0].astype(jnp.float32))
    p = jax.nn.softmax(s, axis=-1)
    o = jnp.einsum('bhtm,bmhd->bthd', p, mkv[:, :, 1].astype(jnp.float32))
    return o.reshape(B, T, M_HEADS * M_HEAD_DIM)


def layer(h, pos, mkv, a_buf, b_buf, c_buf, d_buf, norm_g, w_in, a_qn, a_kn, a_sink, b_cw, b_cb,
          b_lg, b_lb, b_pw, c_wg, c_sc, d_qn, d_kn, m_qn, w_out):
    B, T = h.shape[0], h.shape[1]
    hd = HEAD_DIM
    z = rmsnorm(h, norm_g) @ w_in
    aq, ak, av, bz, cu, dq, dk, dv, mq, gates = jnp.split(z, _in_offsets(), axis=-1)
    aq = rope(rmsnorm(aq.reshape(B, T, A_HEADS, hd), a_qn), pos)
    ak = rope(rmsnorm(ak.reshape(B, T, A_KV_HEADS, hd), a_kn), pos)
    ya, a_new = sliding_attn(aq, ak, av.reshape(B, T, A_KV_HEADS, hd), pos, a_buf, a_sink)
    yb, b_new = conformer_conv(bz, b_buf, b_cw, b_cb, b_lg, b_lb, b_pw)
    yc, c_new = pool_mixer(cu, c_buf, pos, c_wg, c_sc)
    dq = rope(rmsnorm(dq.reshape(B, T, D_HEADS, hd), d_qn), pos)
    dk = rope(rmsnorm(dk.reshape(B, T, D_HEADS, hd), d_kn), pos)
    dv = dv.reshape(B, T, D_HEADS, hd)
    if d_buf is None:
        yd, d_new = dilated_prompt(dq, dk, dv)
    else:
        yd, d_new = dilated_sample(dq, dk, dv, pos, d_buf)
    ym = mem_attn(mq.reshape(B, T, M_HEADS, M_HEAD_DIM), mkv, m_qn)
    y = jnp.concatenate([ya, yb, yc, yd, ym], axis=-1).astype(h.dtype) * jax.nn.silu(gates)
    return h + y @ w_out, a_new, b_new, c_new, d_new


def setup_inputs(seed: int = 0) -> dict:
    key = jax.random.key(seed)
    ks = jax.random.split(key, 27)
    f32 = jnp.float32

    def nrm(k, shape, scale=1.0):
        return jax.random.normal(k, shape, f32) * scale

    def gain(k, shape):
        return 1.0 + 0.1 * jax.random.normal(k, shape, f32)

    L_D = min(D_WINDOW, PAST_LEN)
    return {
        'x_prompt': nrm(ks[0], (BATCH, SEQ, D_MODEL)),
        'x_sample': nrm(ks[1], (DEC_BATCH, DEC_SEQ, D_MODEL)),
        'cache_a_kv': nrm(ks[2], (DEPTH, DEC_BATCH, A_WINDOW, 2, A_KV_HEADS, HEAD_DIM)),
        'state_b_conv': nrm(ks[3], (DEPTH, DEC_BATCH, B_CONV - 1, B_CH), 0.5),
        'state_c_pool': nrm(ks[4], (DEPTH, DEC_BATCH, C_STATE, C_CH)),
        'cache_d_kv': nrm(ks[5], (DEPTH, DEC_BATCH, L_D, 2, D_HEADS, HEAD_DIM)),
        'cache_mem_kv': nrm(ks[6], (DEPTH, DEC_BATCH, N_MEM, 2, M_HEADS, M_HEAD_DIM)),
        'mem_prompt': nrm(ks[7], (BATCH, N_MEM, D_MODEL)),
        'norm_g': gain(ks[8], (DEPTH, D_MODEL)),
        'w_in': nrm(ks[9], (DEPTH, D_MODEL, IN_W), D_MODEL ** -0.5),
        'a_q_norm': gain(ks[10], (DEPTH, HEAD_DIM)),
        'a_k_norm': gain(ks[11], (DEPTH, HEAD_DIM)),
        'a_sinks': nrm(ks[12], (DEPTH, A_HEADS)),
        'b_conv_w': nrm(ks[13], (DEPTH, B_CONV, B_CH), B_CONV ** -0.5),
        'b_conv_b': nrm(ks[14], (DEPTH, B_CH), 0.02),
        'b_ln_g': gain(ks[15], (DEPTH, B_CH)),
        'b_ln_b': nrm(ks[16], (DEPTH, B_CH), 0.02),
        'b_w_pw': nrm(ks[17], (DEPTH, B_CH, B_CH), B_CH ** -0.5),
        'c_w_group': nrm(ks[18], (DEPTH, len(C_WINDOWS), C_GROUP, C_GROUP), C_GROUP ** -0.5),
        'c_scale': gain(ks[19], (DEPTH, C_CH)),
        'd_q_norm': gain(ks[20], (DEPTH, HEAD_DIM)),
        'd_k_norm': gain(ks[21], (DEPTH, HEAD_DIM)),
        'm_norm_g': gain(ks[22], (DEPTH, D_MODEL)),
        'm_w_kv': nrm(ks[23], (DEPTH, D_MODEL, 2 * M_HEADS * M_HEAD_DIM), D_MODEL ** -0.5),
        'm_q_norm': gain(ks[24], (DEPTH, M_HEAD_DIM)),
        'm_k_norm': gain(ks[25], (DEPTH, M_HEAD_DIM)),
        'w_out': nrm(ks[26], (DEPTH, MIX_W, D_MODEL), 0.5 * MIX_W ** -0.5),
    }


def reference(x_prompt, x_sample, cache_a_kv, state_b_conv, state_c_pool, cache_d_kv, cache_mem_kv,
              mem_prompt, norm_g, w_in, a_q_norm, a_k_norm, a_sinks, b_conv_w, b_conv_b, b_ln_g,
              b_ln_b, b_w_pw, c_w_group, c_scale, d_q_norm, d_k_norm, m_norm_g, m_w_kv, m_q_norm,
              m_k_norm, w_out):
    Bp, S = x_prompt.shape[0], x_prompt.shape[1]
    Ts = x_sample.shape[1]
    dt = x_prompt.dtype
    pos_p = jnp.arange(S, dtype=jnp.int32)
    pos_s = PAST_LEN + jnp.arange(Ts, dtype=jnp.int32)
    zero_a = jnp.zeros((Bp, A_WINDOW, 2, A_KV_HEADS, HEAD_DIM), dt)
    zero_b = jnp.zeros((Bp, B_CONV - 1, B_CH), dt)
    zero_c = jnp.zeros((Bp, C_STATE, C_CH), dt)
    hp, hs = x_prompt, x_sample
    pa, pb, pc, pd, pm, sa, sb, sc, sd = ([] for _ in range(9))
    for l in range(DEPTH):
        lw = (norm_g[l], w_in[l], a_q_norm[l], a_k_norm[l], a_sinks[l], b_conv_w[l], b_conv_b[l],
              b_ln_g[l], b_ln_b[l], b_w_pw[l], c_w_group[l], c_scale[l], d_q_norm[l], d_k_norm[l],
              m_q_norm[l], w_out[l])
        mkv = mem_kv_proj(mem_prompt, m_norm_g[l], m_w_kv[l], m_k_norm[l])
        hp, a1, b1, c1, d1 = layer(hp, pos_p, mkv, zero_a, zero_b, zero_c, None, *lw)
        hs, a2, b2, c2, d2 = layer(hs, pos_s, cache_mem_kv[l], cache_a_kv[l], state_b_conv[l],
                                   state_c_pool[l], cache_d_kv[l], *lw)
        pa.append(a1)
        pb.append(b1)
        pc.append(c1)
        pd.append(d1)
        pm.append(mkv)
        sa.append(a2)
        sb.append(b2)
        sc.append(c2)
        sd.append(d2)
    return (hp, hs, jnp.stack(pa), jnp.stack(pb), jnp.stack(pc), jnp.stack(pd), jnp.stack(pm),
            jnp.stack(sa), jnp.stack(sb), jnp.stack(sc), jnp.stack(sd))
```

```python
import functools

import jax
import jax.numpy as jnp
from jax import lax
from jax.experimental import pallas as pl
from jax.experimental.pallas import tpu as pltpu

F32 = jnp.float32
BF16 = jnp.bfloat16

D_MODEL = 2048
SEQ = 4096
DEPTH = 4
PAST_LEN = 16384
HEAD_DIM = 64
BRANCH_W = 768
N_BRANCH = 5
MIX_W = N_BRANCH * BRANCH_W
A_HEADS = 12
A_KV_HEADS = 4
A_KV_W = A_KV_HEADS * HEAD_DIM
A_WINDOW = 128
B_CONV = 31
C_WINDOWS = (2, 4, 8, 16)
C_GROUP = BRANCH_W // 4
C_STATE = 15
D_PATTERNS = ((128, 1), (512, 4), (2048, 16))
D_WINDOW = 2048
M_HEADS = 4
M_HEAD_DIM = 192
N_MEM = 256
ROT_DIM = 16
ROPE_THETA = 500000.0
BLOCK = 128
EPS = 1e-6
NEG = -1e30
ATTN_SCALE = HEAD_DIM ** -0.5

OFF_AQ, OFF_AK, OFF_AV, OFF_BA, OFF_BB, OFF_CU, OFF_DQ, OFF_DK, OFF_DV, OFF_MQ, OFF_GATE = (
    0, 768, 1024, 1280, 2048, 2816, 3584, 4352, 5120, 5888, 6656)
MIXIN_W = OFF_GATE
IN_W = MIXIN_W + MIX_W

LANES = 128
V7X_VMEM_LIMIT = 56 * 1024 * 1024


def _cparams(sem):
    return pltpu.CompilerParams(dimension_semantics=sem, vmem_limit_bytes=V7X_VMEM_LIMIT)


def _dot(a, b):
    return jnp.dot(a, b, preferred_element_type=F32)


def _dot_nt(a, b):
    return lax.dot_general(a, b, (((1,), (1,)), ((), ())), preferred_element_type=F32)


def _silu(x):
    return x * jax.nn.sigmoid(x)


def _window_spec(rows, cols, origin):
    return pl.BlockSpec((pl.Element(rows), pl.Element(cols)), origin)


def _seg_sum(x, ones_bd):
    hi = x.astype(BF16)
    lo = (x - hi.astype(F32)).astype(BF16)
    return _dot(hi, ones_bd) + _dot(lo, ones_bd)


def _in_proj_kernel(x_ref, g_ref, w_ref, o_ref, xn_ref):
    @pl.when(pl.program_id(1) == 0)
    def _():
        x = x_ref[...]
        ms = jnp.mean(x * x, axis=-1, keepdims=True)
        xn_ref[...] = (x * lax.rsqrt(ms + EPS) * g_ref[...]).astype(BF16)

    o_ref[...] = _dot(xn_ref[...], w_ref[...].astype(BF16))


def _in_proj(x, g, w, l, col0, ncols, tn, tm):
    m = x.shape[0]
    return pl.pallas_call(
        _in_proj_kernel,
        grid=(m // tm, ncols // tn),
        in_specs=[
            pl.BlockSpec((tm, D_MODEL), lambda i, j: (i, 0)),
            pl.BlockSpec((None, 1, D_MODEL), lambda i, j: (l, 0, 0)),
            _window_spec(D_MODEL, tn, lambda i, j: (l * D_MODEL, pl.multiple_of(col0 + j * tn, LANES))),
        ],
        out_specs=pl.BlockSpec((tm, tn), lambda i, j: (i, j)),
        out_shape=jax.ShapeDtypeStruct((m, ncols), F32),
        scratch_shapes=[pltpu.VMEM((tm, D_MODEL), BF16)],
        compiler_params=_cparams(("parallel", "arbitrary")),
        name="in_proj",
    )(x, g, w.reshape(w.shape[0] * D_MODEL, w.shape[2]))


def _norm_rope_chunk(x, gain, ones_bd, cos, sin_lo, sin_hi):
    ms = _seg_sum(x * x, ones_bd) * (1.0 / HEAD_DIM)
    y = x * lax.rsqrt(ms + EPS) * gain
    return y * cos + pltpu.roll(y, LANES - 8, 1) * sin_lo + pltpu.roll(y, 8, 1) * sin_hi


def _prep_kernel(aq_ref, ak_ref, dq_ref, dk_ref, cos_ref, slo_ref, shi_ref, gains_ref, ones_ref,
                 aqn_ref, akn_ref, dqn_ref, dkn_ref):
    cos, slo, shi = cos_ref[...], slo_ref[...], shi_ref[...]
    ones_bd = ones_ref[...]
    for idx, (src, dst) in enumerate(((aq_ref, aqn_ref), (ak_ref, akn_ref), (dq_ref, dqn_ref),
                                      (dk_ref, dkn_ref))):
        gain = gains_ref[idx:idx + 1, :]
        for c in range(src.shape[1] // LANES):
            cols = slice(c * LANES, (c + 1) * LANES)
            dst[:, cols] = _norm_rope_chunk(src[:, cols], gain, ones_bd, cos, slo, shi)


def _prep(zm, tabs, gains, ones_bd, l, tm):
    m = zm.shape[0]
    cos, slo, shi = tabs
    nt = cos.shape[0] // tm
    tab_spec = pl.BlockSpec((tm, LANES), lambda i: (i % nt, 0))

    def seg(off, w):
        return _window_spec(tm, w, lambda i: (i * tm, off))

    return pl.pallas_call(
        _prep_kernel,
        grid=(m // tm,),
        in_specs=[seg(OFF_AQ, BRANCH_W), seg(OFF_AK, A_KV_W), seg(OFF_DQ, BRANCH_W), seg(OFF_DK, BRANCH_W),
                  tab_spec, tab_spec, tab_spec,
                  pl.BlockSpec((None, 4, LANES), lambda i: (l, 0, 0)),
                  pl.BlockSpec((LANES, LANES), lambda i: (0, 0))],
        out_specs=[pl.BlockSpec((tm, BRANCH_W), lambda i: (i, 0)),
                   pl.BlockSpec((tm, A_KV_W), lambda i: (i, 0)),
                   pl.BlockSpec((tm, BRANCH_W), lambda i: (i, 0)),
                   pl.BlockSpec((tm, BRANCH_W), lambda i: (i, 0))],
        out_shape=[jax.ShapeDtypeStruct((m, BRANCH_W), F32), jax.ShapeDtypeStruct((m, A_KV_W), F32),
                   jax.ShapeDtypeStruct((m, BRANCH_W), F32), jax.ShapeDtypeStruct((m, BRANCH_W), F32)],
        compiler_params=_cparams(("parallel",)),
        name="qk_norm_rope",
    )(zm, zm, zm, zm, cos, slo, shi, gains, ones_bd)


def _softmax_attn(qs, kp, kc, vp, vc, valid_p, valid_c, sink):
    sp = jnp.where(valid_p, _dot_nt(qs, kp) * ATTN_SCALE, NEG)
    sc = jnp.where(valid_c, _dot_nt(qs, kc) * ATTN_SCALE, NEG)
    m = jnp.maximum(jnp.max(sp, axis=-1, keepdims=True), jnp.max(sc, axis=-1, keepdims=True))
    if sink is not None:
        m = jnp.maximum(m, sink)
    ep = jnp.exp(sp - m)
    ec = jnp.exp(sc - m)
    den = jnp.sum(ep, axis=-1, keepdims=True) + jnp.sum(ec, axis=-1, keepdims=True)
    if sink is not None:
        den = den + jnp.exp(sink - m)
    o = (_dot(ep.astype(BF16), vp) + _dot(ec.astype(BF16), vc)) / den
    return o, m + jnp.log(den)


def _band_masks(rows, tq, tk, prev_thr):
    rp = lax.broadcasted_iota(jnp.int32, (rows, BLOCK), 0) & (tq - 1)
    jp = lax.broadcasted_iota(jnp.int32, (rows, BLOCK), 1)
    rc = lax.broadcasted_iota(jnp.int32, (rows, tk), 0) & (tq - 1)
    jc = lax.broadcasted_iota(jnp.int32, (rows, tk), 1)
    return jp >= rp + prev_thr, jc <= rc


def _lane_lo():
    return lax.broadcasted_iota(jnp.int32, (1, LANES), 1) < HEAD_DIM


def _a_core(q, kp, kc, vp, vc, prev_thr, sink_ref, l):
    tq, tk = q.shape[0], kc.shape[0]
    group = A_HEADS // A_KV_HEADS
    lo = _lane_lo()
    valid_p, valid_c = _band_masks(group * tq, tq, tk, prev_thr)
    heads = [None] * A_HEADS
    for h in range(A_KV_HEADS):
        kcols = slice((h // 2) * LANES, (h // 2 + 1) * LANES)
        half = h % 2
        keep = lo if half == 0 else jnp.logical_not(lo)
        parts, sinks = [], []
        for g in range(group):
            j = group * h + g
            qc = q[:, (j // 2) * LANES:(j // 2 + 1) * LANES]
            if j % 2 != half:
                qc = pltpu.roll(qc, HEAD_DIM, 1)
            parts.append(jnp.where(keep, qc, 0.0))
            sinks.append(jnp.full((tq, 1), sink_ref[l, j], F32))
        qs = jnp.concatenate(parts, axis=0).astype(BF16)
        o, _ = _softmax_attn(qs, kp[:, kcols].astype(BF16), kc[:, kcols].astype(BF16),
                             vp[:, kcols].astype(BF16), vc[:, kcols].astype(BF16),
                             valid_p, valid_c, jnp.concatenate(sinks, axis=0))
        for g in range(group):
            j = group * h + g
            oj = o[g * tq:(g + 1) * tq]
            if j % 2 != half:
                oj = pltpu.roll(oj, HEAD_DIM, 1)
            heads[j] = oj
    return jnp.concatenate([jnp.where(lo, heads[2 * c], heads[2 * c + 1]) for c in range(A_HEADS // 2)],
                           axis=1)


def _a_prompt_kernel(l, nb, sink_ref, q_ref, kp_ref, kc_ref, vp_ref, vc_ref, gate_ref, u_ref, st_ref):
    i = pl.program_id(1)
    prev_thr = jnp.where(i > 0, 0, 2 * BLOCK)
    y = _a_core(q_ref[...], kp_ref[...], kc_ref[...], vp_ref[...], vc_ref[...], prev_thr, sink_ref, l)
    u_ref[...] = (y * _silu(gate_ref[...])).astype(u_ref.dtype)

    @pl.when(i == nb - 1)
    def _():
        st_ref[:, 0:A_KV_W] = kc_ref[...]
        st_ref[:, A_KV_W:2 * A_KV_W] = vc_ref[...]


def _a_prompt(aqn, akn, zm, zg, sinks, l, nbatch):
    nb = SEQ // BLOCK

    def cur(b, i):
        return b * nb + i

    def prev(b, i):
        return b * nb + jnp.maximum(i - 1, 0)

    return pl.pallas_call(
        functools.partial(_a_prompt_kernel, l, nb),
        grid=(nbatch, nb),
        in_specs=[
            pl.BlockSpec(memory_space=pltpu.SMEM),
            pl.BlockSpec((BLOCK, BRANCH_W), lambda b, i: (cur(b, i), 0)),
            pl.BlockSpec((BLOCK, A_KV_W), lambda b, i: (prev(b, i), 0)),
            pl.BlockSpec((BLOCK, A_KV_W), lambda b, i: (cur(b, i), 0)),
            _window_spec(BLOCK, A_KV_W, lambda b, i: (prev(b, i) * BLOCK, OFF_AV)),
            _window_spec(BLOCK, A_KV_W, lambda b, i: (cur(b, i) * BLOCK, OFF_AV)),
            pl.BlockSpec((BLOCK, BRANCH_W), lambda b, i: (cur(b, i), 0)),
        ],
        out_specs=[pl.BlockSpec((BLOCK, BRANCH_W), lambda b, i: (cur(b, i), 0)),
                   pl.BlockSpec((None, A_WINDOW, 2 * A_KV_W), lambda b, i: (b, 0, 0))],
        out_shape=[jax.ShapeDtypeStruct((nbatch * SEQ, BRANCH_W), BF16),
                   jax.ShapeDtypeStruct((nbatch, A_WINDOW, 2 * A_KV_W), F32)],
        compiler_params=_cparams(("arbitrary", "arbitrary")),
        name="a_prompt",
    )(sinks, aqn, akn, akn, zm, zm, zg)


def _pad_rows(x, rows):
    return jnp.concatenate([x, jnp.zeros((rows - x.shape[0], x.shape[1]), x.dtype)], axis=0)


def _a_sample_kernel(l, sink_ref, q_ref, kn_ref, vn_ref, cache_ref, gate_ref, u_ref, st_ref):
    t = q_ref.shape[0]
    kn, vn = kn_ref[...], vn_ref[...]
    kp = cache_ref[:, 0:A_KV_W]
    vp = cache_ref[:, A_KV_W:2 * A_KV_W]
    y = _a_core(q_ref[...], kp, _pad_rows(kn, BLOCK), vp, _pad_rows(vn, BLOCK), 0, sink_ref, l)
    u_ref[...] = (y * _silu(gate_ref[...])).astype(u_ref.dtype)
    st_ref[0:A_WINDOW - t, :] = cache_ref[t:A_WINDOW, :]
    st_ref[A_WINDOW - t:A_WINDOW, 0:A_KV_W] = kn
    st_ref[A_WINDOW - t:A_WINDOW, A_KV_W:2 * A_KV_W] = vn


def _a_sample(aqn, akn, zm, zg, sinks, cache, l, nbatch, t):
    return pl.pallas_call(
        functools.partial(_a_sample_kernel, l),
        grid=(nbatch,),
        in_specs=[
            pl.BlockSpec(memory_space=pltpu.SMEM),
            pl.BlockSpec((t, BRANCH_W), lambda b: (b, 0)),
            pl.BlockSpec((t, A_KV_W), lambda b: (b, 0)),
            _window_spec(t, A_KV_W, lambda b: (b * t, OFF_AV)),
            pl.BlockSpec((None, None, A_WINDOW, 2 * A_KV_W), lambda b: (l, b, 0, 0)),
            pl.BlockSpec((t, BRANCH_W), lambda b: (b, 0)),
        ],
        out_specs=[pl.BlockSpec((t, BRANCH_W), lambda b: (b, 0)),
                   pl.BlockSpec((None, A_WINDOW, 2 * A_KV_W), lambda b: (b, 0, 0))],
        out_shape=[jax.ShapeDtypeStruct((nbatch * t, BRANCH_W), F32),
                   jax.ShapeDtypeStruct((nbatch, A_WINDOW, 2 * A_KV_W), F32)],
        compiler_params=_cparams(("parallel",)),
        name="a_sample",
    )(sinks, aqn, akn, zm, cache, zg)


B_PAD = 32


def _b_kernel(tb, za_ref, zb_ref, st_ref, cw_ref, cb_ref, lg_ref, lb_ref, pw_ref, gate_ref,
              u_ref, sto_ref, gbuf):
    hist = B_CONV - 1
    first = B_PAD - hist

    @pl.when(pl.program_id(1) == 0)
    def _():
        gbuf[first:B_PAD, :] = st_ref[...]

    gbuf[B_PAD:B_PAD + tb, :] = za_ref[...] * jax.nn.sigmoid(zb_ref[...])
    acc = jnp.zeros((tb, BRANCH_W), F32)
    for j in range(B_CONV):
        acc = acc + cw_ref[j:j + 1, :] * gbuf[first + j:first + j + tb, :]
    yf = acc + cb_ref[...]
    mu = jnp.mean(yf, axis=-1, keepdims=True)
    var = jnp.mean(jnp.square(yf - mu), axis=-1, keepdims=True)
    yn = (yf - mu) * lax.rsqrt(var + EPS) * lg_ref[...] + lb_ref[...]
    out = _dot(_silu(yn).astype(BF16), pw_ref[...].astype(BF16))
    u_ref[...] = (out * _silu(gate_ref[...])).astype(u_ref.dtype)
    tail = gbuf[first + tb:B_PAD + tb, :]
    sto_ref[...] = tail
    gbuf[first:B_PAD, :] = tail


def _b_mixer(zm, zg, state, cw, cb, lg, lb, pw, l, sl, nbatch, t, tb, out_dtype):
    nt = t // tb
    hist = B_CONV - 1

    def vec(arr_rows):
        return pl.BlockSpec((None, arr_rows, BRANCH_W), lambda b, i: (l, 0, 0))

    if sl is None:
        st_spec = pl.BlockSpec((None, hist, BRANCH_W), lambda b, i: (b, 0, 0))
    else:
        st_spec = pl.BlockSpec((None, None, hist, BRANCH_W), lambda b, i: (sl, b, 0, 0))
    return pl.pallas_call(
        functools.partial(_b_kernel, tb),
        grid=(nbatch, nt),
        in_specs=[
            _window_spec(tb, BRANCH_W, lambda b, i: ((b * nt + i) * tb, OFF_BA)),
            _window_spec(tb, BRANCH_W, lambda b, i: ((b * nt + i) * tb, OFF_BB)),
            st_spec,
            vec(B_CONV), vec(1), vec(1), vec(1), vec(BRANCH_W),
            pl.BlockSpec((tb, BRANCH_W), lambda b, i: (b * nt + i, 1)),
        ],
        out_specs=[pl.BlockSpec((tb, BRANCH_W), lambda b, i: (b * nt + i, 0)),
                   pl.BlockSpec((None, hist, BRANCH_W), lambda b, i: (b, 0, 0))],
        out_shape=[jax.ShapeDtypeStruct((nbatch * t, BRANCH_W), out_dtype),
                   jax.ShapeDtypeStruct((nbatch, hist, BRANCH_W), F32)],
        scratch_shapes=[pltpu.VMEM((B_PAD + tb, BRANCH_W), F32)],
        compiler_params=_cparams(("arbitrary", "arbitrary")),
        name="b_conv",
    )(zm, zm, state, cw, cb, lg, lb, pw, zg)


C_PAD = 16


def _c_kernel(tb, pos0, u_ref, st_ref, w_ref, sc_ref, gate_ref, o_ref, sto_ref, ubuf, s2, s4, s8):
    i = pl.program_id(1)

    @pl.when(i == 0)
    def _():
        ubuf[1:C_PAD, :] = st_ref[...]

    u = u_ref[...]
    ubuf[C_PAD:C_PAD + tb, :] = u
    n2, n4, n8 = tb + 14, tb + 12, tb + 8
    s2[0:n2, :] = ubuf[1:1 + n2, :] + ubuf[2:2 + n2, :]
    s4[0:n4, :] = s2[0:n4, :] + s2[2:2 + n4, :]
    s8[0:n8, :] = s4[0:n8, :] + s4[4:4 + n8, :]
    sums = (s2[14:14 + tb, :], s4[12:12 + tb, :], s8[8:8 + tb, :], s8[0:tb, :] + s8[8:8 + tb, :])
    pos1 = (pos0 + i * tb + lax.broadcasted_iota(jnp.int32, (tb, 1), 0) + 1).astype(F32)
    grp = lax.broadcasted_iota(jnp.int32, (1, BRANCH_W), 1) // C_GROUP
    mean = sums[3] / jnp.minimum(pos1, float(C_WINDOWS[3]))
    for gi in (2, 1, 0):
        mean = jnp.where(grp == gi, sums[gi] / jnp.minimum(pos1, float(C_WINDOWS[gi])), mean)
    y = _dot((mean - u).astype(BF16), w_ref[...].astype(BF16)) * sc_ref[...]
    o_ref[...] = (y * _silu(gate_ref[...])).astype(o_ref.dtype)
    tail = ubuf[1 + tb:C_PAD + tb, :]
    sto_ref[...] = tail
    ubuf[1:C_PAD, :] = tail


def _c_mixer(zm, zg, state, wbd, scale, l, sl, nbatch, t, tb, pos0, out_dtype):
    nt = t // tb
    if sl is None:
        st_spec = pl.BlockSpec((None, C_STATE, BRANCH_W), lambda b, i: (b, 0, 0))
    else:
        st_spec = pl.BlockSpec((None, None, C_STATE, BRANCH_W), lambda b, i: (sl, b, 0, 0))
    return pl.pallas_call(
        functools.partial(_c_kernel, tb, pos0),
        grid=(nbatch, nt),
        in_specs=[
            _window_spec(tb, BRANCH_W, lambda b, i: ((b * nt + i) * tb, OFF_CU)),
            st_spec,
            pl.BlockSpec((None, BRANCH_W, BRANCH_W), lambda b, i: (l, 0, 0)),
            pl.BlockSpec((None, 1, BRANCH_W), lambda b, i: (l, 0, 0)),
            pl.BlockSpec((tb, BRANCH_W), lambda b, i: (b * nt + i, 2)),
        ],
        out_specs=[pl.BlockSpec((tb, BRANCH_W), lambda b, i: (b * nt + i, 0)),
                   pl.BlockSpec((None, C_STATE, BRANCH_W), lambda b, i: (b, 0, 0))],
        out_shape=[jax.ShapeDtypeStruct((nbatch * t, BRANCH_W), out_dtype),
                   jax.ShapeDtypeStruct((nbatch, C_STATE, BRANCH_W), F32)],
        scratch_shapes=[pltpu.VMEM((C_PAD + tb, BRANCH_W), F32)] * 4,
        compiler_params=_cparams(("arbitrary", "arbitrary")),
        name="c_pool",
    )(zm, state, wbd, scale, zg)


def _d_prompt_kernel(q_ref, kp_ref, kc_ref, vp_ref, vc_ref, o_ref, lse_ref):
    prev_thr = jnp.where(pl.program_id(2) > 0, 0, 2 * BLOCK)
    lo = _lane_lo()
    valid_p, valid_c = _band_masks(2 * BLOCK, BLOCK, BLOCK, prev_thr)
    for c in range(BRANCH_W // LANES):
        cols = slice(c * LANES, (c + 1) * LANES)
        qc = q_ref[:, cols]
        qs = jnp.concatenate([jnp.where(lo, qc, 0.0), jnp.where(lo, 0.0, qc)], axis=0).astype(BF16)
        o, lse = _softmax_attn(qs, kp_ref[:, cols].astype(BF16), kc_ref[:, cols].astype(BF16),
                               vp_ref[:, cols].astype(BF16), vc_ref[:, cols].astype(BF16),
                               valid_p, valid_c, None)
        o_ref[:, cols] = jnp.where(lo, o[0:BLOCK], o[BLOCK:2 * BLOCK])
        lse_ref[:, cols] = jnp.where(lo, lse[0:BLOCK], lse[BLOCK:2 * BLOCK])


def _d_prompt_pattern(dqn, dkn, zm, d, nbatch):
    rows = dqn.shape[0]
    nb = SEQ // d // BLOCK
    qv = dqn.reshape(rows // d, d * BRANCH_W)
    kv = dkn.reshape(rows // d, d * BRANCH_W)
    zv = zm.reshape(rows // d, d * MIXIN_W)

    def cur(b, r, i):
        return b * nb + i

    def prev(b, r, i):
        return b * nb + jnp.maximum(i - 1, 0)

    def vcol(r):
        return pl.multiple_of(r * MIXIN_W + OFF_DV, LANES)

    o, lse = pl.pallas_call(
        _d_prompt_kernel,
        grid=(nbatch, d, nb),
        in_specs=[
            pl.BlockSpec((BLOCK, BRANCH_W), lambda b, r, i: (cur(b, r, i), r)),
            pl.BlockSpec((BLOCK, BRANCH_W), lambda b, r, i: (prev(b, r, i), r)),
            pl.BlockSpec((BLOCK, BRANCH_W), lambda b, r, i: (cur(b, r, i), r)),
            _window_spec(BLOCK, BRANCH_W, lambda b, r, i: (prev(b, r, i) * BLOCK, vcol(r))),
            _window_spec(BLOCK, BRANCH_W, lambda b, r, i: (cur(b, r, i) * BLOCK, vcol(r))),
        ],
        out_specs=[pl.BlockSpec((BLOCK, BRANCH_W), lambda b, r, i: (cur(b, r, i), r))] * 2,
        out_shape=[jax.ShapeDtypeStruct((rows // d, d * BRANCH_W), F32)] * 2,
        compiler_params=_cparams(("parallel", "parallel", "arbitrary")),
        name=f"d_prompt_{d}",
    )(qv, kv, kv, zv, zv)
    return o.reshape(rows, BRANCH_W), lse.reshape(rows, BRANCH_W)


def _d_combine_kernel(o1, l1, o2, l2, o3, l3, gate_ref, u_ref):
    la, lb, lc = l1[...], l2[...], l3[...]
    m = jnp.maximum(jnp.maximum(la, lb), lc)
    ea, eb, ec = jnp.exp(la - m), jnp.exp(lb - m), jnp.exp(lc - m)
    tot = ea + eb + ec
    y = (ea / tot) * o1[...] + (eb / tot) * o2[...] + (ec / tot) * o3[...]
    u_ref[...] = (y * _silu(gate_ref[...])).astype(u_ref.dtype)


def _d_combine(parts, zg, tm):
    rows = zg.shape[0]
    spec = pl.BlockSpec((tm, BRANCH_W), lambda i: (i, 0))
    flat = [a for pair in parts for a in pair]
    return pl.pallas_call(
        _d_combine_kernel,
        grid=(rows // tm,),
        in_specs=[spec] * 6 + [pl.BlockSpec((tm, BRANCH_W), lambda i: (i, 3))],
        out_specs=spec,
        out_shape=jax.ShapeDtypeStruct((rows, BRANCH_W), BF16),
        compiler_params=_cparams(("parallel",)),
        name="d_combine",
    )(*flat, zg)


def _d_sample_kernel(q_ref, kn_ref, vn_ref, cache_ref, gate_ref, u_ref, new_ref):
    t = q_ref.shape[0]
    hist = cache_ref.shape[0]
    ncol = LANES
    q = q_ref[...]
    kn, vn = kn_ref[...], vn_ref[...]
    new_ref[:, 0:BRANCH_W] = kn
    new_ref[:, BRANCH_W:2 * BRANCH_W] = vn

    row_head = lax.broadcasted_iota(jnp.int32, (ncol, BRANCH_W), 0) // t
    lane_head = lax.broadcasted_iota(jnp.int32, (ncol, BRANCH_W), 1) // HEAD_DIM
    qbd = jnp.where(row_head == lane_head, jnp.concatenate([q] * (ncol // t), axis=0), 0.0).astype(BF16)

    kc = cache_ref[:, 0:BRANCH_W].astype(BF16)
    vc = cache_ref[:, BRANCH_W:2 * BRANCH_W].astype(BF16)
    knp = _pad_rows(kn, BLOCK).astype(BF16)
    vnp = _pad_rows(vn, BLOCK).astype(BF16)
    s_c = _dot_nt(kc, qbd) * ATTN_SCALE
    s_n = _dot_nt(knp, qbd) * ATTN_SCALE

    tcol = lax.broadcasted_iota(jnp.int32, (1, ncol), 1) & (t - 1)
    delta_c = hist + tcol - lax.broadcasted_iota(jnp.int32, (hist, ncol), 0)
    delta_n = tcol - lax.broadcasted_iota(jnp.int32, (BLOCK, ncol), 0)
    ecs, ens, dens, lses = [], [], [], []
    for w, d in D_PATTERNS:
        ok_c = ((delta_c & (d - 1)) == 0) & (delta_c <= w)
        ok_n = (delta_n >= 0) & ((delta_n & (d - 1)) == 0)
        sc = jnp.where(ok_c, s_c, NEG)
        sn = jnp.where(ok_n, s_n, NEG)
        m = jnp.maximum(jnp.max(sc, axis=0, keepdims=True), jnp.max(sn, axis=0, keepdims=True))
        ec = jnp.exp(sc - m)
        en = jnp.exp(sn - m)
        den = jnp.sum(ec, axis=0, keepdims=True) + jnp.sum(en, axis=0, keepdims=True)
        ecs.append(ec)
        ens.append(en)
        dens.append(den)
        lses.append(m + jnp.log(den))
    mm = jnp.maximum(jnp.maximum(lses[0], lses[1]), lses[2])
    wts = [jnp.exp(x - mm) for x in lses]
    tot = wts[0] + wts[1] + wts[2]
    coef = [wts[p] / tot / dens[p] for p in range(3)]
    pc = coef[0] * ecs[0] + coef[1] * ecs[1] + coef[2] * ecs[2]
    pn = coef[0] * ens[0] + coef[1] * ens[1] + coef[2] * ens[2]
    yfull = _dot(pc.T.astype(BF16), vc) + _dot(pn.T.astype(BF16), vnp)
    lane_head_t = lax.broadcasted_iota(jnp.int32, (t, BRANCH_W), 1) // HEAD_DIM
    y = jnp.zeros((t, BRANCH_W), F32)
    for h in range(BRANCH_W // HEAD_DIM):
        y = y + jnp.where(lane_head_t == h, yfull[h * t:(h + 1) * t, :], 0.0)
    u_ref[...] = (y * _silu(gate_ref[...])).astype(u_ref.dtype)


def _d_sample(dqn, dkn, zm, zg, cache, l, nbatch, t):
    hist = cache.shape[2]
    return pl.pallas_call(
        _d_sample_kernel,
        grid=(nbatch,),
        in_specs=[
            pl.BlockSpec((t, BRANCH_W), lambda b: (b, 0)),
            pl.BlockSpec((t, BRANCH_W), lambda b: (b, 0)),
            _window_spec(t, BRANCH_W, lambda b: (b * t, OFF_DV)),
            pl.BlockSpec((None, None, hist, 2 * BRANCH_W), lambda b: (l, b, 0, 0)),
            pl.BlockSpec((t, BRANCH_W), lambda b: (b, 3)),
        ],
        out_specs=[pl.BlockSpec((t, BRANCH_W), lambda b: (b, 0)),
                   pl.BlockSpec((t, 2 * BRANCH_W), lambda b: (b, 0))],
        out_shape=[jax.ShapeDtypeStruct((nbatch * t, BRANCH_W), F32),
                   jax.ShapeDtypeStruct((nbatch * t, 2 * BRANCH_W), F32)],
        compiler_params=_cparams(("parallel",)),
        name="d_sample",
    )(dqn, dkn, zm, cache, zg)


M_WIN = 2 * LANES
M_WIN_START = (0, 128, 384, 512)


def _m_head_mask(h):
    lane = lax.broadcasted_iota(jnp.int32, (1, M_WIN), 1)
    first = h * M_HEAD_DIM - M_WIN_START[h]
    return (lane >= first) & (lane < first + M_HEAD_DIM)


def _mem_kv_kernel(x_ref, g_ref, w_ref, kg_ref, ones_ref, o_ref):
    x = x_ref[...]
    ms = jnp.mean(x * x, axis=-1, keepdims=True)
    xn = (x * lax.rsqrt(ms + EPS) * g_ref[...]).astype(BF16)
    kv = _dot(xn, w_ref[...].astype(BF16))
    k = kv[:, 0:BRANCH_W]
    kms = _seg_sum(k * k, ones_ref[...]) * (1.0 / M_HEAD_DIM)
    o_ref[:, 0:BRANCH_W] = k * lax.rsqrt(kms + EPS) * kg_ref[...]
    o_ref[:, BRANCH_W:2 * BRANCH_W] = kv[:, BRANCH_W:2 * BRANCH_W]


def _mem_kv(mem, g, w, kg, ones_bd, l):
    nbatch = mem.shape[0]
    return pl.pallas_call(
        _mem_kv_kernel,
        grid=(nbatch,),
        in_specs=[
            pl.BlockSpec((None, N_MEM, D_MODEL), lambda b: (b, 0, 0)),
            pl.BlockSpec((None, 1, D_MODEL), lambda b: (l, 0, 0)),
            pl.BlockSpec((None, D_MODEL, 2 * BRANCH_W), lambda b: (l, 0, 0)),
            pl.BlockSpec((None, 1, BRANCH_W), lambda b: (l, 0, 0)),
            pl.BlockSpec((BRANCH_W, BRANCH_W), lambda b: (0, 0)),
        ],
        out_specs=pl.BlockSpec((None, N_MEM, 2 * BRANCH_W), lambda b: (b, 0, 0)),
        out_shape=jax.ShapeDtypeStruct((nbatch, N_MEM, 2 * BRANCH_W), F32),
        compiler_params=_cparams(("parallel",)),
        name="mem_kv",
    )(mem, g, w, kg, ones_bd)


def _m_attn_kernel(q_ref, kv_ref, gate_ref, qg_ref, ones_ref, u_ref):
    q = q_ref[...]
    ms = _seg_sum(q * q, ones_ref[...]) * (1.0 / M_HEAD_DIM)
    qn = (q * lax.rsqrt(ms + EPS) * qg_ref[...]) * (M_HEAD_DIM ** -0.5)
    outs = []
    for h in range(M_HEADS):
        win = slice(M_WIN_START[h], M_WIN_START[h] + M_WIN)
        qh = jnp.where(_m_head_mask(h), qn[:, win], 0.0).astype(BF16)
        s = _dot_nt(qh, kv_ref[:, win].astype(BF16))
        e = jnp.exp(s - jnp.max(s, axis=-1, keepdims=True))
        p = e / jnp.sum(e, axis=-1, keepdims=True)
        vwin = slice(BRANCH_W + M_WIN_START[h], BRANCH_W + M_WIN_START[h] + M_WIN)
        outs.append(_dot(p.astype(BF16), kv_ref[:, vwin].astype(BF16)))
    lo = _lane_lo()
    y = jnp.concatenate([
        outs[0][:, 0:LANES],
        jnp.where(lo, outs[0][:, LANES:M_WIN], outs[1][:, 0:LANES]),
        outs[1][:, LANES:M_WIN],
        outs[2][:, 0:LANES],
        jnp.where(lo, outs[2][:, LANES:M_WIN], outs[3][:, 0:LANES]),
        outs[3][:, LANES:M_WIN],
    ], axis=1)
    u_ref[...] = (y * _silu(gate_ref[...])).astype(u_ref.dtype)


def _m_attn(zm, zg, mkv, qg, ones_bd, l, sl, nbatch, t, tq, out_dtype):
    nt = t // tq
    if sl is None:
        kv_spec = pl.BlockSpec((None, N_MEM, 2 * BRANCH_W), lambda b, i: (b, 0, 0))
    else:
        kv_spec = pl.BlockSpec((None, None, N_MEM, 2 * BRANCH_W), lambda b, i: (sl, b, 0, 0))
    return pl.pallas_call(
        _m_attn_kernel,
        grid=(nbatch, nt),
        in_specs=[
            _window_spec(tq, BRANCH_W, lambda b, i: ((b * nt + i) * tq, OFF_MQ)),
            kv_spec,
            pl.BlockSpec((tq, BRANCH_W), lambda b, i: (b * nt + i, 4)),
            pl.BlockSpec((None, 1, BRANCH_W), lambda b, i: (l, 0, 0)),
            pl.BlockSpec((BRANCH_W, BRANCH_W), lambda b, i: (0, 0)),
        ],
        out_specs=pl.BlockSpec((tq, BRANCH_W), lambda b, i: (b * nt + i, 0)),
        out_shape=jax.ShapeDtypeStruct((nbatch * t, BRANCH_W), out_dtype),
        compiler_params=_cparams(("parallel", "parallel")),
        name="m_attn",
    )(zm, mkv, zg, qg, ones_bd)


def _out_proj_kernel(ua, ub, uc, ud, um, w_ref, h_ref, o_ref):
    acc = h_ref[...]
    for g, u_ref in enumerate((ua, ub, uc, ud, um)):
        wg = w_ref[g * BRANCH_W:(g + 1) * BRANCH_W, :].astype(BF16)
        acc = acc + _dot(u_ref[...].astype(BF16), wg)
    o_ref[...] = acc


def _out_proj(us, w, h, l, tm, tn):
    m = h.shape[0]
    u_spec = pl.BlockSpec((tm, BRANCH_W), lambda i, j: (i, 0))
    return pl.pallas_call(
        _out_proj_kernel,
        grid=(m // tm, D_MODEL // tn),
        in_specs=[u_spec] * N_BRANCH + [
            pl.BlockSpec((None, MIX_W, tn), lambda i, j: (l, 0, j)),
            pl.BlockSpec((tm, tn), lambda i, j: (i, j)),
        ],
        out_specs=pl.BlockSpec((tm, tn), lambda i, j: (i, j)),
        out_shape=jax.ShapeDtypeStruct((m, D_MODEL), F32),
        compiler_params=_cparams(("parallel", "parallel")),
        name="out_proj",
    )(*us, w, h)


def _rope_tables(pos):
    half = ROT_DIM // 2
    inv = ROPE_THETA ** (-jnp.arange(half, dtype=F32) / half)
    ang = pos.astype(F32)[:, None] * inv[None, :]
    cos, sin = jnp.cos(ang), jnp.sin(ang)
    n = pos.shape[0]
    rest = HEAD_DIM - ROT_DIM
    one_head = lambda a, b, fill: jnp.concatenate([a, b, jnp.full((n, rest), fill, F32)], axis=1)
    zero = jnp.zeros_like(sin)
    cos_t = one_head(cos, cos, 1.0)
    slo_t = one_head(-sin, zero, 0.0)
    shi_t = one_head(zero, sin, 0.0)
    return tuple(jnp.concatenate([a, a], axis=1) for a in (cos_t, slo_t, shi_t))


def _block_ones(width, seg):
    idx = jnp.arange(width) // seg
    return (idx[:, None] == idx[None, :]).astype(BF16)


def _tile_lanes(x, width):
    return jnp.tile(x, (1, width // x.shape[-1]))


def kernel(x_prompt, x_sample, cache_a_kv, state_b_conv, state_c_pool, cache_d_kv, cache_mem_kv, mem_prompt,
           norm_g, w_in, a_q_norm, a_k_norm, a_sinks, b_conv_w, b_conv_b, b_ln_g, b_ln_b, b_w_pw, c_w_group,
           c_scale, d_q_norm, d_k_norm, m_norm_g, m_w_kv, m_q_norm, m_k_norm, w_out):
    bp, seq = x_prompt.shape[0], x_prompt.shape[1]
    bs, ts = x_sample.shape[0], x_sample.shape[1]
    depth = w_in.shape[0]
    hist_d = cache_d_kv.shape[2]

    hp = x_prompt.reshape(bp * seq, D_MODEL)
    hs = x_sample.reshape(bs * ts, D_MODEL)

    tabs_p = _rope_tables(jnp.arange(seq, dtype=jnp.int32))
    tabs_s = tuple(jnp.tile(a, (bs, 1)) for a in _rope_tables(PAST_LEN + jnp.arange(ts, dtype=jnp.int32)))
    ones64 = _block_ones(LANES, HEAD_DIM)
    ones192 = _block_ones(BRANCH_W, M_HEAD_DIM)

    row = lambda a: a.reshape(depth, 1, a.shape[-1])
    norm_g3, m_norm_g3 = row(norm_g), row(m_norm_g)
    qk_gains = jnp.stack([_tile_lanes(g, LANES) for g in (a_q_norm, a_k_norm, d_q_norm, d_k_norm)], axis=1)
    m_qg, m_kg = row(_tile_lanes(m_q_norm, BRANCH_W)), row(_tile_lanes(m_k_norm, BRANCH_W))
    b_cb, b_lg, b_lb, c_sc = row(b_conv_b), row(b_ln_g), row(b_ln_b), row(c_scale)
    c_wbd = jnp.zeros((depth, BRANCH_W, BRANCH_W), F32)
    for gi in range(len(C_WINDOWS)):
        sl = slice(gi * C_GROUP, (gi + 1) * C_GROUP)
        c_wbd = c_wbd.at[:, sl, sl].set(c_w_group[:, gi])

    cache_a = cache_a_kv.reshape(depth, bs, A_WINDOW, 2 * A_KV_W)
    cache_d = cache_d_kv.reshape(depth, bs, hist_d, 2 * BRANCH_W)
    cache_m = cache_mem_kv.reshape(depth, bs, N_MEM, 2 * BRANCH_W)
    zero_b = jnp.zeros((bp, B_CONV - 1, BRANCH_W), F32)
    zero_c = jnp.zeros((bp, C_STATE, BRANCH_W), F32)

    pa, pb, pc, pd, pm, sa, sb, sc, sd_new = ([] for _ in range(9))
    for l in range(depth):
        mkv = _mem_kv(mem_prompt, m_norm_g3, m_w_kv, m_kg, ones192, l)
        pm.append(mkv)

        zm = _in_proj(hp, norm_g3, w_in, l, 0, MIXIN_W, 512, 1024)
        zg = _in_proj(hp, norm_g3, w_in, l, OFF_GATE, MIX_W, 768, 1024)
        aqn, akn, dqn, dkn = _prep(zm, tabs_p, qk_gains, ones64, l, 512)
        ua, a_st = _a_prompt(aqn, akn, zm, zg, a_sinks, l, bp)
        ub, b_st = _b_mixer(zm, zg, zero_b, b_conv_w, b_cb, b_lg, b_lb, b_w_pw, l, None, bp, seq, 512, BF16)
        uc, c_st = _c_mixer(zm, zg, zero_c, c_wbd, c_sc, l, None, bp, seq, 512, 0, BF16)
        parts = [_d_prompt_pattern(dqn, dkn, zm, d, bp) for _, d in D_PATTERNS]
        ud = _d_combine(parts, zg, 512)
        um = _m_attn(zm, zg, mkv, m_qg, ones192, l, None, bp, seq, 512, BF16)
        hp = _out_proj((ua, ub, uc, ud, um), w_out, hp, l, 1024, 512)
        pa.append(a_st)
        pb.append(b_st)
        pc.append(c_st)
        keep = min(D_WINDOW, seq)
        pd.append(jnp.concatenate(
            [dkn.reshape(bp, seq, BRANCH_W)[:, seq - keep:],
             zm.reshape(bp, seq, MIXIN_W)[:, seq - keep:, OFF_DV:OFF_DV + BRANCH_W]], axis=-1))

        rows_s = bs * ts
        zm = _in_proj(hs, norm_g3, w_in, l, 0, MIXIN_W, 512, rows_s)
        zg = _in_proj(hs, norm_g3, w_in, l, OFF_GATE, MIX_W, 768, rows_s)
        aqn, akn, dqn, dkn = _prep(zm, tabs_s, qk_gains, ones64, l, rows_s)
        ua, a_st = _a_sample(aqn, akn, zm, zg, a_sinks, cache_a, l, bs, ts)
        ub, b_st = _b_mixer(zm, zg, state_b_conv, b_conv_w, b_cb, b_lg, b_lb, b_w_pw, l, l, bs, ts, ts, F32)
        uc, c_st = _c_mixer(zm, zg, state_c_pool, c_wbd, c_sc, l, l, bs, ts, ts, PAST_LEN, F32)
        ud, d_new = _d_sample(dqn, dkn, zm, zg, cache_d, l, bs, ts)
        um = _m_attn(zm, zg, cache_m, m_qg, ones192, l, l, bs, ts, ts, F32)
        hs = _out_proj((ua, ub, uc, ud, um), w_out, hs, l, rows_s, 512)
        sa.append(a_st)
        sb.append(b_st)
        sc.append(c_st)
        sd_new.append(d_new)

    kv5 = lambda a, heads, hd: a.reshape(a.shape[:-1] + (2, heads, hd))
    d_new_all = kv5(jnp.stack(sd_new).reshape(depth, bs, ts, 2 * BRANCH_W), BRANCH_W // HEAD_DIM, HEAD_DIM)
    s_d_kv = jnp.concatenate([cache_d_kv[:, :, ts:], d_new_all], axis=2)
    return (hp.reshape(bp, seq, D_MODEL), hs.reshape(bs, ts, D_MODEL),
            kv5(jnp.stack(pa), A_KV_HEADS, HEAD_DIM), jnp.stack(pb), jnp.stack(pc),
            kv5(jnp.stack(pd), BRANCH_W // HEAD_DIM, HEAD_DIM), kv5(jnp.stack(pm), M_HEADS, M_HEAD_DIM),
            kv5(jnp.stack(sa), A_KV_HEADS, HEAD_DIM), jnp.stack(sb), jnp.stack(sc), s_d_kv)
```

```python
import functools

import jax
import jax.numpy as jnp
from jax import lax
from jax.experimental import pallas as pl
from jax.experimental.pallas import tpu as pltpu

F32 = jnp.float32
BF16 = jnp.bfloat16

D_MODEL = 2048
SEQ = 4096
DEPTH = 4
PAST_LEN = 16384
HEAD_DIM = 64
BRANCH_W = 768
N_BRANCH = 5
MIX_W = N_BRANCH * BRANCH_W
A_HEADS = 12
A_KV_HEADS = 4
A_KV_W = A_KV_HEADS * HEAD_DIM
A_WINDOW = 128
B_CONV = 31
C_WINDOWS = (2, 4, 8, 16)
C_GROUP = BRANCH_W // 4
C_STATE = 15
D_PATTERNS = ((128, 1), (512, 4), (2048, 16))
D_WINDOW = 2048
M_HEADS = 4
M_HEAD_DIM = 192
N_MEM = 256
ROT_DIM = 16
ROPE_THETA = 500000.0
BLOCK = 128
EPS = 1e-6
NEG = -1e30
ATTN_SCALE = HEAD_DIM ** -0.5

OFF_AQ, OFF_AK, OFF_AV, OFF_BA, OFF_BB, OFF_CU, OFF_DQ, OFF_DK, OFF_DV, OFF_MQ, OFF_GATE = (
    0, 768, 1024, 1280, 2048, 2816, 3584, 4352, 5120, 5888, 6656)
MIXIN_W = OFF_GATE
IN_W = MIXIN_W + MIX_W

LANES = 128
SUBLANES = 8
PROJ_ROWS = 2048
V7X_VMEM_LIMIT = 56 * 1024 * 1024


def _cparams(sem):
    return pltpu.CompilerParams(dimension_semantics=sem, vmem_limit_bytes=V7X_VMEM_LIMIT)


def _dot(a, b):
    return jnp.dot(a, b, preferred_element_type=F32)


def _dot_nt(a, b):
    return lax.dot_general(a, b, (((1,), (1,)), ((), ())), preferred_element_type=F32)


def _silu(x):
    return x * jax.nn.sigmoid(x)


def _window_spec(rows, cols, origin):
    return pl.BlockSpec((pl.Element(rows), pl.Element(cols)), origin)


def _seg_sum(x, ones_bd):
    hi = x.astype(BF16)
    lo = (x - hi.astype(F32)).astype(BF16)
    return _dot(hi, ones_bd) + _dot(lo, ones_bd)


def _rms_norm_kernel(x_ref, g_ref, o_ref):
    x = x_ref[...]
    ms = jnp.mean(x * x, axis=-1, keepdims=True)
    o_ref[...] = (x * lax.rsqrt(ms + EPS) * g_ref[...]).astype(o_ref.dtype)


def _rms_norm(x, g, l, tm):
    m = x.shape[0]
    return pl.pallas_call(
        _rms_norm_kernel,
        grid=(m // tm,),
        in_specs=[pl.BlockSpec((tm, D_MODEL), lambda i: (i, 0)),
                  pl.BlockSpec((None, 1, D_MODEL), lambda i: (l, 0, 0))],
        out_specs=pl.BlockSpec((tm, D_MODEL), lambda i: (i, 0)),
        out_shape=jax.ShapeDtypeStruct((m, D_MODEL), BF16),
        compiler_params=_cparams(("parallel",)),
        name="rms_norm",
    )(x, g)


def _in_proj_kernel(x_ref, w_ref, o_ref):
    o_ref[...] = _dot(x_ref[...], w_ref[...].astype(BF16))


def _in_proj(xn, w, l, col0, ncols, tn, tm):
    m = xn.shape[0]
    return pl.pallas_call(
        _in_proj_kernel,
        grid=(m // tm, ncols // tn),
        in_specs=[
            pl.BlockSpec((tm, D_MODEL), lambda i, j: (i, 0), pipeline_mode=pl.Buffered(1)),
            _window_spec(D_MODEL, tn, lambda i, j: (l * D_MODEL, pl.multiple_of(col0 + j * tn, LANES))),
        ],
        out_specs=pl.BlockSpec((tm, tn), lambda i, j: (i, j)),
        out_shape=jax.ShapeDtypeStruct((m, ncols), F32),
        compiler_params=_cparams(("parallel", "parallel")),
        name="in_proj",
    )(xn, w.reshape(w.shape[0] * D_MODEL, w.shape[2]))


def _norm_rope_chunk(x, gain, ones_bd, cos, sin_lo, sin_hi):
    ms = _seg_sum(x * x, ones_bd) * (1.0 / HEAD_DIM)
    y = x * lax.rsqrt(ms + EPS) * gain
    return y * cos + pltpu.roll(y, LANES - 8, 1) * sin_lo + pltpu.roll(y, 8, 1) * sin_hi


def _prep_kernel(aq_ref, ak_ref, dq_ref, dk_ref, cos_ref, slo_ref, shi_ref, gains_ref, ones_ref,
                 aqn_ref, akn_ref, dqn_ref, dkn_ref):
    cos, slo, shi = cos_ref[...], slo_ref[...], shi_ref[...]
    ones_bd = ones_ref[...]
    for idx, (src, dst) in enumerate(((aq_ref, aqn_ref), (ak_ref, akn_ref), (dq_ref, dqn_ref),
                                      (dk_ref, dkn_ref))):
        gain = gains_ref[idx:idx + 1, :]
        for c in range(src.shape[1] // LANES):
            cols = slice(c * LANES, (c + 1) * LANES)
            dst[:, cols] = _norm_rope_chunk(src[:, cols], gain, ones_bd, cos, slo, shi)


def _prep(zm, tabs, gains, ones_bd, l, tm):
    m = zm.shape[0]
    cos, slo, shi = tabs
    nt = cos.shape[0] // tm
    tab_spec = pl.BlockSpec((tm, LANES), lambda i: (i % nt, 0))

    def seg(off, w):
        return _window_spec(tm, w, lambda i: (i * tm, off))

    return pl.pallas_call(
        _prep_kernel,
        grid=(m // tm,),
        in_specs=[seg(OFF_AQ, BRANCH_W), seg(OFF_AK, A_KV_W), seg(OFF_DQ, BRANCH_W), seg(OFF_DK, BRANCH_W),
                  tab_spec, tab_spec, tab_spec,
                  pl.BlockSpec((None, 4, LANES), lambda i: (l, 0, 0)),
                  pl.BlockSpec((LANES, LANES), lambda i: (0, 0))],
        out_specs=[pl.BlockSpec((tm, BRANCH_W), lambda i: (i, 0)),
                   pl.BlockSpec((tm, A_KV_W), lambda i: (i, 0)),
                   pl.BlockSpec((tm, BRANCH_W), lambda i: (i, 0)),
                   pl.BlockSpec((tm, BRANCH_W), lambda i: (i, 0))],
        out_shape=[jax.ShapeDtypeStruct((m, BRANCH_W), F32), jax.ShapeDtypeStruct((m, A_KV_W), F32),
                   jax.ShapeDtypeStruct((m, BRANCH_W), F32), jax.ShapeDtypeStruct((m, BRANCH_W), F32)],
        compiler_params=_cparams(("parallel",)),
        name="qk_norm_rope",
    )(zm, zm, zm, zm, cos, slo, shi, gains, ones_bd)


def _softmax_attn(qs, kp, kc, vp, vc, valid_p, valid_c, sink, prev_bias=None):
    sp = _dot_nt(qs, kp) * ATTN_SCALE
    if prev_bias is not None:
        sp = sp + prev_bias
    sp = jnp.where(valid_p, sp, NEG)
    sc = jnp.where(valid_c, _dot_nt(qs, kc) * ATTN_SCALE, NEG)
    m = jnp.max(jnp.maximum(sp, sc), axis=-1, keepdims=True)
    if sink is not None:
        m = jnp.maximum(m, sink)
    ep = jnp.exp(sp - m)
    ec = jnp.exp(sc - m)
    den = jnp.sum(ep + ec, axis=-1, keepdims=True)
    if sink is not None:
        den = den + jnp.exp(sink - m)
    o = (_dot(ep.astype(BF16), vp) + _dot(ec.astype(BF16), vc)) / den
    return o, m + jnp.log(den)


def _band_masks(rows, tq, tk, prev_thr):
    rp = lax.broadcasted_iota(jnp.int32, (rows, BLOCK), 0) & (tq - 1)
    jp = lax.broadcasted_iota(jnp.int32, (rows, BLOCK), 1)
    rc = lax.broadcasted_iota(jnp.int32, (rows, tk), 0) & (tq - 1)
    jc = lax.broadcasted_iota(jnp.int32, (rows, tk), 1)
    return jp >= rp + prev_thr, jc <= rc


def _lane_lo():
    return lax.broadcasted_iota(jnp.int32, (1, LANES), 1) < HEAD_DIM


def _a_core(q, kp, kc, vp, vc, prev_thr, sink_ref, l):
    tq, tk = q.shape[0], kc.shape[0]
    group = A_HEADS // A_KV_HEADS
    lo = _lane_lo()
    valid_p, valid_c = _band_masks(group * tq, tq, tk, prev_thr)
    heads = [None] * A_HEADS
    for h in range(A_KV_HEADS):
        kcols = slice((h // 2) * LANES, (h // 2 + 1) * LANES)
        half = h % 2
        keep = lo if half == 0 else jnp.logical_not(lo)
        parts, sinks = [], []
        for g in range(group):
            j = group * h + g
            qc = q[:, (j // 2) * LANES:(j // 2 + 1) * LANES]
            if j % 2 != half:
                qc = pltpu.roll(qc, HEAD_DIM, 1)
            parts.append(jnp.where(keep, qc, 0.0))
            sinks.append(jnp.full((tq, 1), sink_ref[l, j], F32))
        qs = jnp.concatenate(parts, axis=0).astype(BF16)
        o, _ = _softmax_attn(qs, kp[:, kcols].astype(BF16), kc[:, kcols].astype(BF16),
                             vp[:, kcols].astype(BF16), vc[:, kcols].astype(BF16),
                             valid_p, valid_c, jnp.concatenate(sinks, axis=0))
        for g in range(group):
            j = group * h + g
            oj = o[g * tq:(g + 1) * tq]
            if j % 2 != half:
                oj = pltpu.roll(oj, HEAD_DIM, 1)
            heads[j] = oj
    return jnp.concatenate([jnp.where(lo, heads[2 * c], heads[2 * c + 1]) for c in range(A_HEADS // 2)],
                           axis=1)


def _a_prompt_kernel(l, nb, sink_ref, q_ref, kp_ref, kc_ref, vp_ref, vc_ref, gate_ref, u_ref, st_ref):
    i = pl.program_id(1)
    prev_thr = jnp.where(i > 0, 0, 2 * BLOCK)
    y = _a_core(q_ref[...], kp_ref[...], kc_ref[...], vp_ref[...], vc_ref[...], prev_thr, sink_ref, l)
    u_ref[...] = (y * _silu(gate_ref[...])).astype(u_ref.dtype)

    @pl.when(i == nb - 1)
    def _():
        st_ref[:, 0:A_KV_W] = kc_ref[...]
        st_ref[:, A_KV_W:2 * A_KV_W] = vc_ref[...]


def _a_prompt(aqn, akn, zm, zg, sinks, l, nbatch):
    nb = SEQ // BLOCK

    def cur(b, i):
        return b * nb + i

    def prev(b, i):
        return b * nb + jnp.maximum(i - 1, 0)

    return pl.pallas_call(
        functools.partial(_a_prompt_kernel, l, nb),
        grid=(nbatch, nb),
        in_specs=[
            pl.BlockSpec(memory_space=pltpu.SMEM),
            pl.BlockSpec((BLOCK, BRANCH_W), lambda b, i: (cur(b, i), 0)),
            pl.BlockSpec((BLOCK, A_KV_W), lambda b, i: (prev(b, i), 0)),
            pl.BlockSpec((BLOCK, A_KV_W), lambda b, i: (cur(b, i), 0)),
            _window_spec(BLOCK, A_KV_W, lambda b, i: (prev(b, i) * BLOCK, OFF_AV)),
            _window_spec(BLOCK, A_KV_W, lambda b, i: (cur(b, i) * BLOCK, OFF_AV)),
            pl.BlockSpec((BLOCK, BRANCH_W), lambda b, i: (cur(b, i), 0)),
        ],
        out_specs=[pl.BlockSpec((BLOCK, BRANCH_W), lambda b, i: (cur(b, i), 0)),
                   pl.BlockSpec((None, A_WINDOW, 2 * A_KV_W), lambda b, i: (b, 0, 0))],
        out_shape=[jax.ShapeDtypeStruct((nbatch * SEQ, BRANCH_W), BF16),
                   jax.ShapeDtypeStruct((nbatch, A_WINDOW, 2 * A_KV_W), F32)],
        compiler_params=_cparams(("arbitrary", "arbitrary")),
        name="a_prompt",
    )(sinks, aqn, akn, akn, zm, zm, zg)


def _pad_rows(x, rows):
    return jnp.concatenate([x, jnp.zeros((rows - x.shape[0], x.shape[1]), x.dtype)], axis=0)


def _a_sample_kernel(l, sink_ref, q_ref, kn_ref, vn_ref, cache_ref, gate_ref, u_ref, st_ref):
    t = q_ref.shape[0]
    kn, vn = kn_ref[...], vn_ref[...]
    kp = cache_ref[:, 0:A_KV_W]
    vp = cache_ref[:, A_KV_W:2 * A_KV_W]
    y = _a_core(q_ref[...], kp, _pad_rows(kn, BLOCK), vp, _pad_rows(vn, BLOCK), 0, sink_ref, l)
    u_ref[...] = (y * _silu(gate_ref[...])).astype(u_ref.dtype)
    st_ref[0:A_WINDOW - t, :] = cache_ref[t:A_WINDOW, :]
    st_ref[A_WINDOW - t:A_WINDOW, 0:A_KV_W] = kn
    st_ref[A_WINDOW - t:A_WINDOW, A_KV_W:2 * A_KV_W] = vn


def _a_sample(aqn, akn, zm, zg, sinks, cache, l, nbatch, t):
    return pl.pallas_call(
        functools.partial(_a_sample_kernel, l),
        grid=(nbatch,),
        in_specs=[
            pl.BlockSpec(memory_space=pltpu.SMEM),
            pl.BlockSpec((t, BRANCH_W), lambda b: (b, 0)),
            pl.BlockSpec((t, A_KV_W), lambda b: (b, 0)),
            _window_spec(t, A_KV_W, lambda b: (b * t, OFF_AV)),
            pl.BlockSpec((None, None, A_WINDOW, 2 * A_KV_W), lambda b: (l, b, 0, 0)),
            pl.BlockSpec((t, BRANCH_W), lambda b: (b, 0)),
        ],
        out_specs=[pl.BlockSpec((t, BRANCH_W), lambda b: (b, 0)),
                   pl.BlockSpec((None, A_WINDOW, 2 * A_KV_W), lambda b: (b, 0, 0))],
        out_shape=[jax.ShapeDtypeStruct((nbatch * t, BRANCH_W), F32),
                   jax.ShapeDtypeStruct((nbatch, A_WINDOW, 2 * A_KV_W), F32)],
        compiler_params=_cparams(("parallel",)),
        name="a_sample",
    )(sinks, aqn, akn, zm, cache, zg)


B_PAD = 32
B_CONV_ROWS = 32


def _b_kernel(tb, za_ref, zb_ref, st_ref, cw_ref, cb_ref, lg_ref, lb_ref, pw_ref, gate_ref,
              u_ref, sto_ref, gbuf, ybuf, sbuf, wtile):
    hist = B_CONV - 1
    first = B_PAD - hist

    @pl.when(pl.program_id(1) == 0)
    def _():
        gbuf[first:B_PAD, :] = st_ref[...]

    gbuf[B_PAD:B_PAD + tb, :] = za_ref[...] * jax.nn.sigmoid(zb_ref[...])
    rc = min(tb, B_CONV_ROWS)

    span = tb + B_PAD - SUBLANES
    for s in range(1, SUBLANES):
        sbuf[s - 1] = gbuf[s:s + span, :]
    for j in range(B_CONV):
        wtile[j] = jnp.broadcast_to(cw_ref[j:j + 1, :], (SUBLANES, BRANCH_W))
    for base in range(0, tb, rc):
        acc = jnp.zeros((rc // SUBLANES, SUBLANES, BRANCH_W), F32)
        for j in range(B_CONV):
            whole, s = divmod(first + j, SUBLANES)
            row0 = base + whole * SUBLANES
            rows = gbuf[row0:row0 + rc, :] if s == 0 else sbuf[s - 1, row0:row0 + rc, :]
            acc = acc + wtile[j][None] * rows.reshape(rc // SUBLANES, SUBLANES, BRANCH_W)
        ybuf[base:base + rc, :] = acc.reshape(rc, BRANCH_W)
    yf = ybuf[...] + cb_ref[...]
    mu = jnp.mean(yf, axis=-1, keepdims=True)
    var = jnp.mean(jnp.square(yf - mu), axis=-1, keepdims=True)
    yn = (yf - mu) * lax.rsqrt(var + EPS) * lg_ref[...] + lb_ref[...]
    out = _dot(_silu(yn).astype(BF16), pw_ref[...].astype(BF16))
    u_ref[...] = (out * _silu(gate_ref[...])).astype(u_ref.dtype)
    tail = gbuf[first + tb:B_PAD + tb, :]
    sto_ref[...] = tail
    gbuf[first:B_PAD, :] = tail


def _b_mixer(zm, zg, state, cw, cb, lg, lb, pw, l, sl, nbatch, t, tb, out_dtype):
    nt = t // tb
    hist = B_CONV - 1

    def vec(arr_rows):
        return pl.BlockSpec((None, arr_rows, BRANCH_W), lambda b, i: (l, 0, 0))

    if sl is None:
        st_spec = pl.BlockSpec((None, hist, BRANCH_W), lambda b, i: (b, 0, 0))
    else:
        st_spec = pl.BlockSpec((None, None, hist, BRANCH_W), lambda b, i: (sl, b, 0, 0))
    return pl.pallas_call(
        functools.partial(_b_kernel, tb),
        grid=(nbatch, nt),
        in_specs=[
            _window_spec(tb, BRANCH_W, lambda b, i: ((b * nt + i) * tb, OFF_BA)),
            _window_spec(tb, BRANCH_W, lambda b, i: ((b * nt + i) * tb, OFF_BB)),
            st_spec,
            vec(B_CONV), vec(1), vec(1), vec(1), vec(BRANCH_W),
            pl.BlockSpec((tb, BRANCH_W), lambda b, i: (b * nt + i, 1)),
        ],
        out_specs=[pl.BlockSpec((tb, BRANCH_W), lambda b, i: (b * nt + i, 0)),
                   pl.BlockSpec((None, hist, BRANCH_W), lambda b, i: (b, 0, 0))],
        out_shape=[jax.ShapeDtypeStruct((nbatch * t, BRANCH_W), out_dtype),
                   jax.ShapeDtypeStruct((nbatch, hist, BRANCH_W), F32)],
        scratch_shapes=[pltpu.VMEM((B_PAD + tb, BRANCH_W), F32), pltpu.VMEM((tb, BRANCH_W), F32),
                        pltpu.VMEM((SUBLANES - 1, tb + B_PAD - SUBLANES, BRANCH_W), F32),
                        pltpu.VMEM((B_CONV, SUBLANES, BRANCH_W), F32)],
        compiler_params=_cparams(("arbitrary", "arbitrary")),
        name="b_conv",
    )(zm, zm, state, cw, cb, lg, lb, pw, zg)


C_PAD = 16


def _c_kernel(tb, pos0, u_ref, st_ref, w_ref, sc_ref, gate_ref, o_ref, sto_ref, ubuf, s2, s4, s8):
    i = pl.program_id(1)

    @pl.when(i == 0)
    def _():
        ubuf[1:C_PAD, :] = st_ref[...]

    u = u_ref[...]
    ubuf[C_PAD:C_PAD + tb, :] = u
    n2, n4, n8 = tb + 14, tb + 12, tb + 8
    s2[0:n2, :] = ubuf[1:1 + n2, :] + ubuf[2:2 + n2, :]
    s4[0:n4, :] = s2[0:n4, :] + s2[2:2 + n4, :]
    s8[0:n8, :] = s4[0:n8, :] + s4[4:4 + n8, :]
    sums = (s2[14:14 + tb, :], s4[12:12 + tb, :], s8[8:8 + tb, :], s8[0:tb, :] + s8[8:8 + tb, :])
    pos1 = (pos0 + i * tb + lax.broadcasted_iota(jnp.int32, (tb, 1), 0) + 1).astype(F32)
    grp = lax.broadcasted_iota(jnp.int32, (1, BRANCH_W), 1) // C_GROUP
    mean = sums[3] / jnp.minimum(pos1, float(C_WINDOWS[3]))
    for gi in (2, 1, 0):
        mean = jnp.where(grp == gi, sums[gi] / jnp.minimum(pos1, float(C_WINDOWS[gi])), mean)
    y = _dot((mean - u).astype(BF16), w_ref[...].astype(BF16)) * sc_ref[...]
    o_ref[...] = (y * _silu(gate_ref[...])).astype(o_ref.dtype)
    tail = ubuf[1 + tb:C_PAD + tb, :]
    sto_ref[...] = tail
    ubuf[1:C_PAD, :] = tail


def _c_mixer(zm, zg, state, wbd, scale, l, sl, nbatch, t, tb, pos0, out_dtype):
    nt = t // tb
    if sl is None:
        st_spec = pl.BlockSpec((None, C_STATE, BRANCH_W), lambda b, i: (b, 0, 0))
    else:
        st_spec = pl.BlockSpec((None, None, C_STATE, BRANCH_W), lambda b, i: (sl, b, 0, 0))
    return pl.pallas_call(
        functools.partial(_c_kernel, tb, pos0),
        grid=(nbatch, nt),
        in_specs=[
            _window_spec(tb, BRANCH_W, lambda b, i: ((b * nt + i) * tb, OFF_CU)),
            st_spec,
            pl.BlockSpec((None, BRANCH_W, BRANCH_W), lambda b, i: (l, 0, 0)),
            pl.BlockSpec((None, 1, BRANCH_W), lambda b, i: (l, 0, 0)),
            pl.BlockSpec((tb, BRANCH_W), lambda b, i: (b * nt + i, 2)),
        ],
        out_specs=[pl.BlockSpec((tb, BRANCH_W), lambda b, i: (b * nt + i, 0)),
                   pl.BlockSpec((None, C_STATE, BRANCH_W), lambda b, i: (b, 0, 0))],
        out_shape=[jax.ShapeDtypeStruct((nbatch * t, BRANCH_W), out_dtype),
                   jax.ShapeDtypeStruct((nbatch, C_STATE, BRANCH_W), F32)],
        scratch_shapes=[pltpu.VMEM((C_PAD + tb, BRANCH_W), F32)] * 4,
        compiler_params=_cparams(("arbitrary", "arbitrary")),
        name="c_pool",
    )(zm, state, wbd, scale, zg)


D_COMBINE_ROWS = 512
D_BLOCK_UNROLL = 4


def _d_prompt_kernel(q_ref, k_ref, v_ref, gate_ref, u_ref, o_s, l_s):
    lo = _lane_lo()
    rp = lax.broadcasted_iota(jnp.int32, (2 * BLOCK, BLOCK), 0) & (BLOCK - 1)
    jp = lax.broadcasted_iota(jnp.int32, (2 * BLOCK, BLOCK), 1)
    valid_c = jp <= rp
    valid_p = jp >= rp
    for p, (_, d) in enumerate(D_PATTERNS):
        nb = SEQ // d // BLOCK
        span = BLOCK * d

        def rows(start):
            return pl.ds(start, BLOCK) if d == 1 else pl.ds(start, BLOCK, stride=d)

        def block(it, carry):
            kp, vp = carry
            r = it // nb
            i = it - r * nb
            cur = r + i * span
            prev_bias = jnp.where(i > 0, 0.0, NEG)
            qc = q_ref[rows(cur), :]
            qs = jnp.concatenate([jnp.where(lo, qc, 0.0), jnp.where(lo, 0.0, qc)], axis=0).astype(BF16)
            kc = k_ref[rows(cur), :].astype(BF16)
            vc = v_ref[rows(cur), :].astype(BF16)
            o, lse = _softmax_attn(qs, kp, kc, vp, vc, valid_p, valid_c, None, prev_bias)
            o_s[p, rows(cur), :] = jnp.where(lo, o[0:BLOCK], o[BLOCK:2 * BLOCK])
            l_s[p, rows(cur), :] = jnp.where(lo, lse[0:BLOCK], lse[BLOCK:2 * BLOCK])
            return kc, vc

        zero = jnp.zeros((BLOCK, LANES), BF16)
        lax.fori_loop(0, d * nb, block, (zero, zero), unroll=D_BLOCK_UNROLL)

    def combine(t, carry):
        sl = pl.ds(pl.multiple_of(t * D_COMBINE_ROWS, D_COMBINE_ROWS), D_COMBINE_ROWS)
        la, lb, lc = l_s[0, sl, :], l_s[1, sl, :], l_s[2, sl, :]
        m = jnp.maximum(jnp.maximum(la, lb), lc)
        ea, eb, ec = jnp.exp(la - m), jnp.exp(lb - m), jnp.exp(lc - m)
        tot = ea + eb + ec
        y = (ea / tot) * o_s[0, sl, :] + (eb / tot) * o_s[1, sl, :] + (ec / tot) * o_s[2, sl, :]
        u_ref[sl, :] = (y * _silu(gate_ref[sl, :])).astype(u_ref.dtype)
        return carry

    lax.fori_loop(0, SEQ // D_COMBINE_ROWS, combine, 0)


def _d_prompt(dqn, dkn, zm, zg, nbatch):
    nchunk = BRANCH_W // LANES
    gate0 = 3 * nchunk
    qk_spec = pl.BlockSpec((SEQ, LANES), lambda b, c: (b, c))
    return pl.pallas_call(
        _d_prompt_kernel,
        grid=(nbatch, nchunk),
        in_specs=[
            qk_spec, qk_spec,
            _window_spec(SEQ, LANES, lambda b, c: (b * SEQ, pl.multiple_of(OFF_DV + c * LANES, LANES))),
            pl.BlockSpec((SEQ, LANES), lambda b, c: (b, gate0 + c)),
        ],
        out_specs=qk_spec,
        out_shape=jax.ShapeDtypeStruct((nbatch * SEQ, BRANCH_W), BF16),
        scratch_shapes=[pltpu.VMEM((len(D_PATTERNS), SEQ, LANES), F32)] * 2,
        compiler_params=_cparams(("parallel", "parallel")),
        name="d_prompt",
    )(dqn, dkn, zm, zg)


D_GROUP = 4
D_GROUP_W = D_GROUP * HEAD_DIM


def _d_sample_kernel(q_ref, kn_ref, vn_ref, gate_ref, cache_ref, *rest):
    u_ref, out_ref = rest[-2], rest[-1]
    t = q_ref.shape[0]
    hist = cache_ref.shape[3]
    nrow = D_GROUP * t
    q = q_ref[...]
    kn, vn = kn_ref[...], vn_ref[...]

    row_head = lax.broadcasted_iota(jnp.int32, (nrow, D_GROUP_W), 0) // t
    lane_head = lax.broadcasted_iota(jnp.int32, (nrow, D_GROUP_W), 1) // HEAD_DIM
    own = row_head == lane_head
    qbd = jnp.where(own, jnp.concatenate([q] * D_GROUP, axis=0), 0.0).astype(BF16)

    k_t = cache_ref[0].reshape(D_GROUP_W, hist)
    v_t = cache_ref[1].reshape(D_GROUP_W, hist)
    knp = _pad_rows(kn, BLOCK)
    vnp = _pad_rows(vn, BLOCK)
    s_c = _dot(qbd, k_t.astype(BF16)) * ATTN_SCALE
    s_n = _dot_nt(qbd, knp.astype(BF16)) * ATTN_SCALE

    trow_c = lax.broadcasted_iota(jnp.int32, (nrow, hist), 0) & (t - 1)
    delta_c = hist + trow_c - lax.broadcasted_iota(jnp.int32, (nrow, hist), 1)
    trow_n = lax.broadcasted_iota(jnp.int32, (nrow, BLOCK), 0) & (t - 1)
    delta_n = trow_n - lax.broadcasted_iota(jnp.int32, (nrow, BLOCK), 1)
    ecs, ens, dens, lses = [], [], [], []
    for w, d in D_PATTERNS:
        ok_c = ((delta_c & (d - 1)) == 0) & (delta_c <= w)
        ok_n = (delta_n >= 0) & ((delta_n & (d - 1)) == 0)
        sc = jnp.where(ok_c, s_c, NEG)
        sn = jnp.where(ok_n, s_n, NEG)
        m = jnp.maximum(jnp.max(sc, axis=-1, keepdims=True), jnp.max(sn, axis=-1, keepdims=True))
        ec = jnp.exp(sc - m)
        en = jnp.exp(sn - m)
        den = jnp.sum(ec, axis=-1, keepdims=True) + jnp.sum(en, axis=-1, keepdims=True)
        ecs.append(ec)
        ens.append(en)
        dens.append(den)
        lses.append(m + jnp.log(den))
    mm = jnp.maximum(jnp.maximum(lses[0], lses[1]), lses[2])
    wts = [jnp.exp(x - mm) for x in lses]
    tot = wts[0] + wts[1] + wts[2]
    coef = [wts[p] / tot / dens[p] for p in range(3)]
    pc = coef[0] * ecs[0] + coef[1] * ecs[1] + coef[2] * ecs[2]
    pn = coef[0] * ens[0] + coef[1] * ens[1] + coef[2] * ens[2]
    yfull = _dot_nt(pc.astype(BF16), v_t.astype(BF16)) + _dot(pn.astype(BF16), vnp.astype(BF16))
    y = jnp.zeros((t, D_GROUP_W), F32)
    for j in range(D_GROUP):
        y = y + jnp.where(own[j * t:(j + 1) * t], yfull[j * t:(j + 1) * t], 0.0)
    u_ref[...] = (y * _silu(gate_ref[...])).astype(u_ref.dtype)

    lane = lax.broadcasted_iota(jnp.int32, (1, LANES), 1)
    is_new = lane >= LANES - t
    for kv, new in ((0, knp), (1, vnp)):
        new_t = pltpu.roll(new.T, LANES - t, 1)
        old = cache_ref[kv].reshape(D_GROUP_W, hist)
        shifted = pltpu.roll(old, hist - t, 1)
        out_ref[kv, :, :, 0:hist - LANES] = shifted[:, 0:hist - LANES].reshape(D_GROUP, HEAD_DIM, hist - LANES)
        tail = jnp.where(is_new, new_t, shifted[:, hist - LANES:hist])
        out_ref[kv, :, :, hist - LANES:hist] = tail.reshape(D_GROUP, HEAD_DIM, LANES)


def _d_sample(dqn, dkn, zm, zg, cache_t, prev_out, l, nbatch, t):
    ngroup = cache_t.shape[3] // D_GROUP
    gate0 = 3 * BRANCH_W // D_GROUP_W
    cache_spec = pl.BlockSpec((None, None, 2, D_GROUP, HEAD_DIM, cache_t.shape[5]),
                              lambda b, g: (l, b, 0, g, 0, 0))
    small = pl.BlockSpec((t, D_GROUP_W), lambda b, g: (b, g))
    in_specs = [small, small,
                _window_spec(t, D_GROUP_W, lambda b, g: (b * t, pl.multiple_of(OFF_DV + g * D_GROUP_W, LANES))),
                pl.BlockSpec((t, D_GROUP_W), lambda b, g: (b, gate0 + g)),
                cache_spec]
    args = [dqn, dkn, zm, zg, cache_t]
    aliases = {}
    if prev_out is not None:
        in_specs.append(pl.BlockSpec(memory_space=pl.ANY))
        args.append(prev_out)
        aliases = {len(args) - 1: 1}
    return pl.pallas_call(
        _d_sample_kernel,
        grid=(nbatch, ngroup),
        in_specs=in_specs,
        out_specs=[small, cache_spec],
        out_shape=[jax.ShapeDtypeStruct((nbatch * t, BRANCH_W), F32),
                   jax.ShapeDtypeStruct(cache_t.shape, F32)],
        input_output_aliases=aliases,
        compiler_params=_cparams(("parallel", "parallel")),
        name="d_sample",
    )(*args)


M_WIN = 2 * LANES
M_WIN_START = (0, 128, 384, 512)


def _m_head_mask(h):
    lane = lax.broadcasted_iota(jnp.int32, (1, M_WIN), 1)
    first = h * M_HEAD_DIM - M_WIN_START[h]
    return (lane >= first) & (lane < first + M_HEAD_DIM)


def _mem_kv_kernel(x_ref, g_ref, w_ref, kg_ref, ones_ref, o_ref):
    x = x_ref[...]
    ms = jnp.mean(x * x, axis=-1, keepdims=True)
    xn = (x * lax.rsqrt(ms + EPS) * g_ref[...]).astype(BF16)
    kv = _dot(xn, w_ref[...].astype(BF16))
    k = kv[:, 0:BRANCH_W]
    kms = _seg_sum(k * k, ones_ref[...]) * (1.0 / M_HEAD_DIM)
    o_ref[:, 0:BRANCH_W] = k * lax.rsqrt(kms + EPS) * kg_ref[...]
    o_ref[:, BRANCH_W:2 * BRANCH_W] = kv[:, BRANCH_W:2 * BRANCH_W]


def _mem_kv(mem, g, w, kg, ones_bd, l):
    nbatch = mem.shape[0]
    return pl.pallas_call(
        _mem_kv_kernel,
        grid=(nbatch,),
        in_specs=[
            pl.BlockSpec((None, N_MEM, D_MODEL), lambda b: (b, 0, 0)),
            pl.BlockSpec((None, 1, D_MODEL), lambda b: (l, 0, 0)),
            pl.BlockSpec((None, D_MODEL, 2 * BRANCH_W), lambda b: (l, 0, 0)),
            pl.BlockSpec((None, 1, BRANCH_W), lambda b: (l, 0, 0)),
            pl.BlockSpec((BRANCH_W, BRANCH_W), lambda b: (0, 0)),
        ],
        out_specs=pl.BlockSpec((None, N_MEM, 2 * BRANCH_W), lambda b: (b, 0, 0)),
        out_shape=jax.ShapeDtypeStruct((nbatch, N_MEM, 2 * BRANCH_W), F32),
        compiler_params=_cparams(("parallel",)),
        name="mem_kv",
    )(mem, g, w, kg, ones_bd)


def _m_attn_kernel(q_ref, kv_ref, gate_ref, qg_ref, ones_ref, u_ref):
    q = q_ref[...]
    ms = _seg_sum(q * q, ones_ref[...]) * (1.0 / M_HEAD_DIM)
    qn = (q * lax.rsqrt(ms + EPS) * qg_ref[...]) * (M_HEAD_DIM ** -0.5)
    outs = []
    for h in range(M_HEADS):
        win = slice(M_WIN_START[h], M_WIN_START[h] + M_WIN)
        qh = jnp.where(_m_head_mask(h), qn[:, win], 0.0).astype(BF16)
        s = _dot_nt(qh, kv_ref[:, win].astype(BF16))
        e = jnp.exp(s - jnp.max(s, axis=-1, keepdims=True))
        p = e / jnp.sum(e, axis=-1, keepdims=True)
        vwin = slice(BRANCH_W + M_WIN_START[h], BRANCH_W + M_WIN_START[h] + M_WIN)
        outs.append(_dot(p.astype(BF16), kv_ref[:, vwin].astype(BF16)))
    lo = _lane_lo()
    y = jnp.concatenate([
        outs[0][:, 0:LANES],
        jnp.where(lo, outs[0][:, LANES:M_WIN], outs[1][:, 0:LANES]),
        outs[1][:, LANES:M_WIN],
        outs[2][:, 0:LANES],
        jnp.where(lo, outs[2][:, LANES:M_WIN], outs[3][:, 0:LANES]),
        outs[3][:, LANES:M_WIN],
    ], axis=1)
    u_ref[...] = (y * _silu(gate_ref[...])).astype(u_ref.dtype)


def _m_attn(zm, zg, mkv, qg, ones_bd, l, sl, nbatch, t, tq, out_dtype):
    nt = t // tq
    if sl is None:
        kv_spec = pl.BlockSpec((None, N_MEM, 2 * BRANCH_W), lambda b, i: (b, 0, 0))
    else:
        kv_spec = pl.BlockSpec((None, None, N_MEM, 2 * BRANCH_W), lambda b, i: (sl, b, 0, 0))
    return pl.pallas_call(
        _m_attn_kernel,
        grid=(nbatch, nt),
        in_specs=[
            _window_spec(tq, BRANCH_W, lambda b, i: ((b * nt + i) * tq, OFF_MQ)),
            kv_spec,
            pl.BlockSpec((tq, BRANCH_W), lambda b, i: (b * nt + i, 4)),
            pl.BlockSpec((None, 1, BRANCH_W), lambda b, i: (l, 0, 0)),
            pl.BlockSpec((BRANCH_W, BRANCH_W), lambda b, i: (0, 0)),
        ],
        out_specs=pl.BlockSpec((tq, BRANCH_W), lambda b, i: (b * nt + i, 0)),
        out_shape=jax.ShapeDtypeStruct((nbatch * t, BRANCH_W), out_dtype),
        compiler_params=_cparams(("parallel", "parallel")),
        name="m_attn",
    )(zm, mkv, zg, qg, ones_bd)


def _out_proj_kernel(ua, ub, uc, ud, um, w_ref, h_ref, o_ref):
    acc = h_ref[...]
    for g, u_ref in enumerate((ua, ub, uc, ud, um)):
        wg = w_ref[g * BRANCH_W:(g + 1) * BRANCH_W, :].astype(BF16)
        acc = acc + _dot(u_ref[...].astype(BF16), wg)
    o_ref[...] = acc


def _out_proj(us, w, h, l, tm, tn):
    m = h.shape[0]
    u_spec = pl.BlockSpec((tm, BRANCH_W), lambda i, j: (i, 0), pipeline_mode=pl.Buffered(1))
    return pl.pallas_call(
        _out_proj_kernel,
        grid=(m // tm, D_MODEL // tn),
        in_specs=[u_spec] * N_BRANCH + [
            pl.BlockSpec((None, MIX_W, tn), lambda i, j: (l, 0, j)),
            pl.BlockSpec((tm, tn), lambda i, j: (i, j)),
        ],
        out_specs=pl.BlockSpec((tm, tn), lambda i, j: (i, j)),
        out_shape=jax.ShapeDtypeStruct((m, D_MODEL), F32),
        compiler_params=_cparams(("parallel", "parallel")),
        name="out_proj",
    )(*us, w, h)


def _rope_tables(pos):
    half = ROT_DIM // 2
    inv = ROPE_THETA ** (-jnp.arange(half, dtype=F32) / half)
    ang = pos.astype(F32)[:, None] * inv[None, :]
    cos, sin = jnp.cos(ang), jnp.sin(ang)
    n = pos.shape[0]
    rest = HEAD_DIM - ROT_DIM
    one_head = lambda a, b, fill: jnp.concatenate([a, b, jnp.full((n, rest), fill, F32)], axis=1)
    zero = jnp.zeros_like(sin)
    cos_t = one_head(cos, cos, 1.0)
    slo_t = one_head(-sin, zero, 0.0)
    shi_t = one_head(zero, sin, 0.0)
    return tuple(jnp.concatenate([a, a], axis=1) for a in (cos_t, slo_t, shi_t))


def _block_ones(width, seg):
    idx = jnp.arange(width) // seg
    return (idx[:, None] == idx[None, :]).astype(BF16)


def _tile_lanes(x, width):
    return jnp.tile(x, (1, width // x.shape[-1]))


def kernel(x_prompt, x_sample, cache_a_kv, state_b_conv, state_c_pool, cache_d_kv, cache_mem_kv, mem_prompt,
           norm_g, w_in, a_q_norm, a_k_norm, a_sinks, b_conv_w, b_conv_b, b_ln_g, b_ln_b, b_w_pw, c_w_group,
           c_scale, d_q_norm, d_k_norm, m_norm_g, m_w_kv, m_q_norm, m_k_norm, w_out):
    bp, seq = x_prompt.shape[0], x_prompt.shape[1]
    bs, ts = x_sample.shape[0], x_sample.shape[1]
    depth = w_in.shape[0]
    hist_d = cache_d_kv.shape[2]

    hp = x_prompt.reshape(bp * seq, D_MODEL)
    hs = x_sample.reshape(bs * ts, D_MODEL)

    tabs_p = _rope_tables(jnp.arange(seq, dtype=jnp.int32))
    tabs_s = tuple(jnp.tile(a, (bs, 1)) for a in _rope_tables(PAST_LEN + jnp.arange(ts, dtype=jnp.int32)))
    ones64 = _block_ones(LANES, HEAD_DIM)
    ones192 = _block_ones(BRANCH_W, M_HEAD_DIM)

    row = lambda a: a.reshape(depth, 1, a.shape[-1])
    norm_g3, m_norm_g3 = row(norm_g), row(m_norm_g)
    qk_gains = jnp.stack([_tile_lanes(g, LANES) for g in (a_q_norm, a_k_norm, d_q_norm, d_k_norm)], axis=1)
    m_qg, m_kg = row(_tile_lanes(m_q_norm, BRANCH_W)), row(_tile_lanes(m_k_norm, BRANCH_W))
    b_cb, b_lg, b_lb, c_sc = row(b_conv_b), row(b_ln_g), row(b_ln_b), row(c_scale)
    c_wbd = jnp.zeros((depth, BRANCH_W, BRANCH_W), F32)
    for gi in range(len(C_WINDOWS)):
        sl = slice(gi * C_GROUP, (gi + 1) * C_GROUP)
        c_wbd = c_wbd.at[:, sl, sl].set(c_w_group[:, gi])

    cache_a = cache_a_kv.reshape(depth, bs, A_WINDOW, 2 * A_KV_W)
    cache_d_t = jnp.transpose(cache_d_kv, (0, 1, 3, 4, 5, 2))
    sd_t = None
    cache_m = cache_mem_kv.reshape(depth, bs, N_MEM, 2 * BRANCH_W)
    zero_b = jnp.zeros((bp, B_CONV - 1, BRANCH_W), F32)
    zero_c = jnp.zeros((bp, C_STATE, BRANCH_W), F32)

    pa, pb, pc, pd, pm, sa, sb, sc = ([] for _ in range(8))
    for l in range(depth):
        mkv = _mem_kv(mem_prompt, m_norm_g3, m_w_kv, m_kg, ones192, l)
        pm.append(mkv)

        xn = _rms_norm(hp, norm_g3, l, 512)
        zm = _in_proj(xn, w_in, l, 0, MIXIN_W, 512, PROJ_ROWS)
        zg = _in_proj(xn, w_in, l, OFF_GATE, MIX_W, 768, PROJ_ROWS)
        aqn, akn, dqn, dkn = _prep(zm, tabs_p, qk_gains, ones64, l, 512)
        ua, a_st = _a_prompt(aqn, akn, zm, zg, a_sinks, l, bp)
        ub, b_st = _b_mixer(zm, zg, zero_b, b_conv_w, b_cb, b_lg, b_lb, b_w_pw, l, None, bp, seq, 512, BF16)
        uc, c_st = _c_mixer(zm, zg, zero_c, c_wbd, c_sc, l, None, bp, seq, 512, 0, BF16)
        ud = _d_prompt(dqn, dkn, zm, zg, bp)
        um = _m_attn(zm, zg, mkv, m_qg, ones192, l, None, bp, seq, 512, BF16)
        hp = _out_proj((ua, ub, uc, ud, um), w_out, hp, l, PROJ_ROWS, 256)
        pa.append(a_st)
        pb.append(b_st)
        pc.append(c_st)
        keep = min(D_WINDOW, seq)
        pd.append(jnp.concatenate(
            [dkn.reshape(bp, seq, BRANCH_W)[:, seq - keep:],
             zm.reshape(bp, seq, MIXIN_W)[:, seq - keep:, OFF_DV:OFF_DV + BRANCH_W]], axis=-1))

        rows_s = bs * ts
        xn = _rms_norm(hs, norm_g3, l, rows_s)
        zm = _in_proj(xn, w_in, l, 0, MIXIN_W, 512, rows_s)
        zg = _in_proj(xn, w_in, l, OFF_GATE, MIX_W, 768, rows_s)
        aqn, akn, dqn, dkn = _prep(zm, tabs_s, qk_gains, ones64, l, rows_s)
        ua, a_st = _a_sample(aqn, akn, zm, zg, a_sinks, cache_a, l, bs, ts)
        ub, b_st = _b_mixer(zm, zg, state_b_conv, b_conv_w, b_cb, b_lg, b_lb, b_w_pw, l, l, bs, ts, ts, F32)
        uc, c_st = _c_mixer(zm, zg, state_c_pool, c_wbd, c_sc, l, l, bs, ts, ts, PAST_LEN, F32)
        ud, sd_t = _d_sample(dqn, dkn, zm, zg, cache_d_t, sd_t, l, bs, ts)
        um = _m_attn(zm, zg, cache_m, m_qg, ones192, l, l, bs, ts, ts, F32)
        hs = _out_proj((ua, ub, uc, ud, um), w_out, hs, l, rows_s, 512)
        sa.append(a_st)
        sb.append(b_st)
        sc.append(c_st)

    kv5 = lambda a, heads, hd: a.reshape(a.shape[:-1] + (2, heads, hd))
    s_d_kv = jnp.transpose(sd_t, (0, 1, 5, 2, 3, 4))
    return (hp.reshape(bp, seq, D_MODEL), hs.reshape(bs, ts, D_MODEL),
            kv5(jnp.stack(pa), A_KV_HEADS, HEAD_DIM), jnp.stack(pb), jnp.stack(pc),
            kv5(jnp.stack(pd), BRANCH_W // HEAD_DIM, HEAD_DIM), kv5(jnp.stack(pm), M_HEADS, M_HEAD_DIM),
            kv5(jnp.stack(sa), A_KV_HEADS, HEAD_DIM), jnp.stack(sb), jnp.stack(sc), s_d_kv)
```

```python
import functools

import jax
import jax.numpy as jnp
from jax import lax
from jax.experimental import pallas as pl
from jax.experimental.pallas import tpu as pltpu

F32 = jnp.float32
BF16 = jnp.bfloat16

D_MODEL = 2048
SEQ = 4096
DEPTH = 4
PAST_LEN = 16384
HEAD_DIM = 64
BRANCH_W = 768
N_BRANCH = 5
MIX_W = N_BRANCH * BRANCH_W
A_HEADS = 12
A_KV_HEADS = 4
A_KV_W = A_KV_HEADS * HEAD_DIM
A_WINDOW = 128
B_CONV = 31
C_WINDOWS = (2, 4, 8, 16)
C_GROUP = BRANCH_W // 4
C_STATE = 15
D_PATTERNS = ((128, 1), (512, 4), (2048, 16))
D_WINDOW = 2048
M_HEADS = 4
M_HEAD_DIM = 192
N_MEM = 256
ROT_DIM = 16
ROPE_THETA = 500000.0
BLOCK = 128
EPS = 1e-6
NEG = -1e30
ATTN_SCALE = HEAD_DIM ** -0.5

OFF_AQ, OFF_AK, OFF_AV, OFF_BA, OFF_BB, OFF_CU, OFF_DQ, OFF_DK, OFF_DV, OFF_MQ, OFF_GATE = (
    0, 768, 1024, 1280, 2048, 2816, 3584, 4352, 5120, 5888, 6656)
MIXIN_W = OFF_GATE
IN_W = MIXIN_W + MIX_W

LANES = 128
SUBLANES = 8
SAMPLE_GROUP = 4
PROJ_ROWS = 2048
V7X_VMEM_LIMIT = 56 * 1024 * 1024


def _cparams(sem):
    return pltpu.CompilerParams(dimension_semantics=sem, vmem_limit_bytes=V7X_VMEM_LIMIT)


def _dot(a, b):
    return jnp.dot(a, b, preferred_element_type=F32)


def _dot_nt(a, b):
    return lax.dot_general(a, b, (((1,), (1,)), ((), ())), preferred_element_type=F32)


def _silu(x):
    return x * jax.nn.sigmoid(x)


def _window_spec(rows, cols, origin):
    return pl.BlockSpec((pl.Element(rows), pl.Element(cols)), origin)


def _seg_sum(x, ones_bd):
    hi = x.astype(BF16)
    lo = (x - hi.astype(F32)).astype(BF16)
    return _dot(hi, ones_bd) + _dot(lo, ones_bd)


def _rms_norm_kernel(x_ref, g_ref, o_ref):
    x = x_ref[...]
    ms = jnp.mean(x * x, axis=-1, keepdims=True)
    o_ref[...] = (x * lax.rsqrt(ms + EPS) * g_ref[...]).astype(o_ref.dtype)


def _rms_norm(x, g, l, tm):
    m = x.shape[0]
    return pl.pallas_call(
        _rms_norm_kernel,
        grid=(m // tm,),
        in_specs=[pl.BlockSpec((tm, D_MODEL), lambda i: (i, 0)),
                  pl.BlockSpec((None, 1, D_MODEL), lambda i: (l, 0, 0))],
        out_specs=pl.BlockSpec((tm, D_MODEL), lambda i: (i, 0)),
        out_shape=jax.ShapeDtypeStruct((m, D_MODEL), BF16),
        compiler_params=_cparams(("parallel",)),
        name="rms_norm",
    )(x, g)


def _in_proj_kernel(x_ref, w_ref, o_ref):
    o_ref[...] = _dot(x_ref[...], w_ref[...].astype(BF16))


def _in_proj(xn, w, l, col0, ncols, tn, tm):
    m = xn.shape[0]
    return pl.pallas_call(
        _in_proj_kernel,
        grid=(m // tm, ncols // tn),
        in_specs=[
            pl.BlockSpec((tm, D_MODEL), lambda i, j: (i, 0), pipeline_mode=pl.Buffered(1)),
            _window_spec(D_MODEL, tn, lambda i, j: (l * D_MODEL, pl.multiple_of(col0 + j * tn, LANES))),
        ],
        out_specs=pl.BlockSpec((tm, tn), lambda i, j: (i, j)),
        out_shape=jax.ShapeDtypeStruct((m, ncols), F32),
        compiler_params=_cparams(("parallel", "parallel")),
        name="in_proj",
    )(xn, w.reshape(w.shape[0] * D_MODEL, w.shape[2]))


def _norm_rope_chunk(x, gain, ones_bd, cos, sin_lo, sin_hi):
    ms = _seg_sum(x * x, ones_bd) * (1.0 / HEAD_DIM)
    y = x * lax.rsqrt(ms + EPS) * gain
    return y * cos + pltpu.roll(y, LANES - 8, 1) * sin_lo + pltpu.roll(y, 8, 1) * sin_hi


def _prep_kernel(aq_ref, ak_ref, dq_ref, dk_ref, cos_ref, slo_ref, shi_ref, gains_ref, ones_ref,
                 aqn_ref, akn_ref, dqn_ref, dkn_ref):
    cos, slo, shi = cos_ref[...], slo_ref[...], shi_ref[...]
    ones_bd = ones_ref[...]
    for idx, (src, dst) in enumerate(((aq_ref, aqn_ref), (ak_ref, akn_ref), (dq_ref, dqn_ref),
                                      (dk_ref, dkn_ref))):
        gain = gains_ref[idx:idx + 1, :]
        for c in range(src.shape[1] // LANES):
            cols = slice(c * LANES, (c + 1) * LANES)
            dst[:, cols] = _norm_rope_chunk(src[:, cols], gain, ones_bd, cos, slo, shi)


def _prep(zm, tabs, gains, ones_bd, l, tm):
    m = zm.shape[0]
    cos, slo, shi = tabs
    nt = cos.shape[0] // tm
    tab_spec = pl.BlockSpec((tm, LANES), lambda i: (i % nt, 0))

    def seg(off, w):
        return _window_spec(tm, w, lambda i: (i * tm, off))

    return pl.pallas_call(
        _prep_kernel,
        grid=(m // tm,),
        in_specs=[seg(OFF_AQ, BRANCH_W), seg(OFF_AK, A_KV_W), seg(OFF_DQ, BRANCH_W), seg(OFF_DK, BRANCH_W),
                  tab_spec, tab_spec, tab_spec,
                  pl.BlockSpec((None, 4, LANES), lambda i: (l, 0, 0)),
                  pl.BlockSpec((LANES, LANES), lambda i: (0, 0))],
        out_specs=[pl.BlockSpec((tm, BRANCH_W), lambda i: (i, 0)),
                   pl.BlockSpec((tm, A_KV_W), lambda i: (i, 0)),
                   pl.BlockSpec((tm, BRANCH_W), lambda i: (i, 0)),
                   pl.BlockSpec((tm, BRANCH_W), lambda i: (i, 0))],
        out_shape=[jax.ShapeDtypeStruct((m, BRANCH_W), F32), jax.ShapeDtypeStruct((m, A_KV_W), F32),
                   jax.ShapeDtypeStruct((m, BRANCH_W), F32), jax.ShapeDtypeStruct((m, BRANCH_W), F32)],
        compiler_params=_cparams(("parallel",)),
        name="qk_norm_rope",
    )(zm, zm, zm, zm, cos, slo, shi, gains, ones_bd)


def _softmax_attn(qs, kp, kc, vp, vc, valid_p, valid_c, sink, prev_bias=None):
    sp = _dot_nt(qs, kp)
    if prev_bias is not None:
        sp = sp + prev_bias
    sp = jnp.where(valid_p, sp, NEG)
    sc = jnp.where(valid_c, _dot_nt(qs, kc), NEG)
    m = jnp.max(jnp.maximum(sp, sc), axis=-1, keepdims=True)
    if sink is not None:
        m = jnp.maximum(m, sink)
    ep = jnp.exp(sp - m)
    ec = jnp.exp(sc - m)
    den = jnp.sum(ep + ec, axis=-1, keepdims=True)
    if sink is not None:
        den = den + jnp.exp(sink - m)
    o = (_dot(ep.astype(BF16), vp) + _dot(ec.astype(BF16), vc)) / den
    return o, m + jnp.log(den)


def _band_masks(rows, tq, tk, prev_thr):
    rp = lax.broadcasted_iota(jnp.int32, (rows, BLOCK), 0) & (tq - 1)
    jp = lax.broadcasted_iota(jnp.int32, (rows, BLOCK), 1)
    rc = lax.broadcasted_iota(jnp.int32, (rows, tk), 0) & (tq - 1)
    jc = lax.broadcasted_iota(jnp.int32, (rows, tk), 1)
    return jp >= rp + prev_thr, jc <= rc


def _lane_lo():
    return lax.broadcasted_iota(jnp.int32, (1, LANES), 1) < HEAD_DIM


def _a_core(q, kp, kc, vp, vc, prev_thr, sink_ref, l):
    tq, tk = q.shape[0], kc.shape[0]
    group = A_HEADS // A_KV_HEADS
    lo = _lane_lo()
    valid_p, valid_c = _band_masks(group * tq, tq, tk, prev_thr)
    heads = [None] * A_HEADS
    for h in range(A_KV_HEADS):
        kcols = slice((h // 2) * LANES, (h // 2 + 1) * LANES)
        half = h % 2
        keep = lo if half == 0 else jnp.logical_not(lo)
        parts, sinks = [], []
        for g in range(group):
            j = group * h + g
            qc = q[:, (j // 2) * LANES:(j // 2 + 1) * LANES]
            if j % 2 != half:
                qc = pltpu.roll(qc, HEAD_DIM, 1)
            parts.append(jnp.where(keep, qc * ATTN_SCALE, 0.0))
            sinks.append(jnp.full((tq, 1), sink_ref[l, j], F32))
        qs = jnp.concatenate(parts, axis=0).astype(BF16)
        o, _ = _softmax_attn(qs, kp[:, kcols].astype(BF16), kc[:, kcols].astype(BF16),
                             vp[:, kcols].astype(BF16), vc[:, kcols].astype(BF16),
                             valid_p, valid_c, jnp.concatenate(sinks, axis=0))
        for g in range(group):
            j = group * h + g
            oj = o[g * tq:(g + 1) * tq]
            if j % 2 != half:
                oj = pltpu.roll(oj, HEAD_DIM, 1)
            heads[j] = oj
    return jnp.concatenate([jnp.where(lo, heads[2 * c], heads[2 * c + 1]) for c in range(A_HEADS // 2)],
                           axis=1)


def _a_prompt_kernel(l, nb, sink_ref, q_ref, kp_ref, kc_ref, vp_ref, vc_ref, gate_ref, u_ref, st_ref):
    i = pl.program_id(1)
    prev_thr = jnp.where(i > 0, 0, 2 * BLOCK)
    y = _a_core(q_ref[...], kp_ref[...], kc_ref[...], vp_ref[...], vc_ref[...], prev_thr, sink_ref, l)
    u_ref[...] = (y * _silu(gate_ref[...])).astype(u_ref.dtype)

    @pl.when(i == nb - 1)
    def _():
        st_ref[:, 0:A_KV_W] = kc_ref[...]
        st_ref[:, A_KV_W:2 * A_KV_W] = vc_ref[...]


def _a_prompt(aqn, akn, zm, zg, sinks, l, nbatch):
    nb = SEQ // BLOCK

    def cur(b, i):
        return b * nb + i

    def prev(b, i):
        return b * nb + jnp.maximum(i - 1, 0)

    return pl.pallas_call(
        functools.partial(_a_prompt_kernel, l, nb),
        grid=(nbatch, nb),
        in_specs=[
            pl.BlockSpec(memory_space=pltpu.SMEM),
            pl.BlockSpec((BLOCK, BRANCH_W), lambda b, i: (cur(b, i), 0)),
            pl.BlockSpec((BLOCK, A_KV_W), lambda b, i: (prev(b, i), 0)),
            pl.BlockSpec((BLOCK, A_KV_W), lambda b, i: (cur(b, i), 0)),
            _window_spec(BLOCK, A_KV_W, lambda b, i: (prev(b, i) * BLOCK, OFF_AV)),
            _window_spec(BLOCK, A_KV_W, lambda b, i: (cur(b, i) * BLOCK, OFF_AV)),
            pl.BlockSpec((BLOCK, BRANCH_W), lambda b, i: (cur(b, i), 0)),
        ],
        out_specs=[pl.BlockSpec((BLOCK, BRANCH_W), lambda b, i: (cur(b, i), 0)),
                   pl.BlockSpec((None, A_WINDOW, 2 * A_KV_W), lambda b, i: (b, 0, 0))],
        out_shape=[jax.ShapeDtypeStruct((nbatch * SEQ, BRANCH_W), BF16),
                   jax.ShapeDtypeStruct((nbatch, A_WINDOW, 2 * A_KV_W), F32)],
        compiler_params=_cparams(("arbitrary", "arbitrary")),
        name="a_prompt",
    )(sinks, aqn, akn, akn, zm, zm, zg)


def _pad_rows(x, rows):
    return jnp.concatenate([x, jnp.zeros((rows - x.shape[0], x.shape[1]), x.dtype)], axis=0)


def _a_sample_kernel(l, nbb, t, sink_ref, q_ref, kn_ref, vn_ref, cache_ref, gate_ref, u_ref, st_ref):
    ys = []
    for bi in range(nbb):
        rows = slice(bi * t, (bi + 1) * t)
        kn, vn = kn_ref[rows, :], vn_ref[rows, :]
        kp = cache_ref[bi, :, 0:A_KV_W]
        vp = cache_ref[bi, :, A_KV_W:2 * A_KV_W]
        ys.append(_a_core(q_ref[rows, :], kp, _pad_rows(kn, BLOCK), vp, _pad_rows(vn, BLOCK), 0, sink_ref, l))
        st_ref[bi, 0:A_WINDOW - t, :] = cache_ref[bi, t:A_WINDOW, :]
        st_ref[bi, A_WINDOW - t:A_WINDOW, 0:A_KV_W] = kn
        st_ref[bi, A_WINDOW - t:A_WINDOW, A_KV_W:2 * A_KV_W] = vn
    y = jnp.concatenate(ys, axis=0)
    u_ref[...] = (y * _silu(gate_ref[...])).astype(u_ref.dtype)


def _a_sample(aqn, akn, zm, zg, sinks, cache, l, nbatch, t, nbb):
    rows = nbb * t
    return pl.pallas_call(
        functools.partial(_a_sample_kernel, l, nbb, t),
        grid=(nbatch // nbb,),
        in_specs=[
            pl.BlockSpec(memory_space=pltpu.SMEM),
            pl.BlockSpec((rows, BRANCH_W), lambda i: (i, 0)),
            pl.BlockSpec((rows, A_KV_W), lambda i: (i, 0)),
            _window_spec(rows, A_KV_W, lambda i: (i * rows, OFF_AV)),
            pl.BlockSpec((None, nbb, A_WINDOW, 2 * A_KV_W), lambda i: (l, i, 0, 0)),
            pl.BlockSpec((rows, BRANCH_W), lambda i: (i, 0)),
        ],
        out_specs=[pl.BlockSpec((rows, BRANCH_W), lambda i: (i, 0)),
                   pl.BlockSpec((nbb, A_WINDOW, 2 * A_KV_W), lambda i: (i, 0, 0))],
        out_shape=[jax.ShapeDtypeStruct((nbatch * t, BRANCH_W), F32),
                   jax.ShapeDtypeStruct((nbatch, A_WINDOW, 2 * A_KV_W), F32)],
        compiler_params=_cparams(("parallel",)),
        name="a_sample",
    )(sinks, aqn, akn, zm, cache, zg)


B_PAD = 32
B_CONV_ROWS = 32


def _b_kernel(tb, nbb, za_ref, zb_ref, st_ref, cw_ref, cb_ref, lg_ref, lb_ref, pw_ref, gate_ref,
              u_ref, sto_ref, gbuf, ybuf, sbuf, wtile):
    hist = B_CONV - 1
    first = B_PAD - hist
    rc = min(tb, B_CONV_ROWS)
    span = tb + B_PAD - SUBLANES
    for j in range(B_CONV):
        wtile[j] = jnp.broadcast_to(cw_ref[j:j + 1, :], (SUBLANES, BRANCH_W))
    for bi in range(nbb):
        seq = slice(bi * tb, (bi + 1) * tb)

        @pl.when(pl.program_id(1) == 0)
        def _():
            gbuf[first:B_PAD, :] = st_ref[bi]

        gbuf[B_PAD:B_PAD + tb, :] = za_ref[seq, :] * jax.nn.sigmoid(zb_ref[seq, :])
        for s in range(1, SUBLANES):
            sbuf[s - 1] = gbuf[s:s + span, :]
        for base in range(0, tb, rc):
            acc = jnp.zeros((rc // SUBLANES, SUBLANES, BRANCH_W), F32)
            for j in range(B_CONV):
                whole, s = divmod(first + j, SUBLANES)
                row0 = base + whole * SUBLANES
                rows = gbuf[row0:row0 + rc, :] if s == 0 else sbuf[s - 1, row0:row0 + rc, :]
                acc = acc + wtile[j][None] * rows.reshape(rc // SUBLANES, SUBLANES, BRANCH_W)
            ybuf[bi * tb + base:bi * tb + base + rc, :] = acc.reshape(rc, BRANCH_W)
        tail = gbuf[first + tb:B_PAD + tb, :]
        sto_ref[bi] = tail
        gbuf[first:B_PAD, :] = tail
    yf = ybuf[...] + cb_ref[...]
    mu = jnp.mean(yf, axis=-1, keepdims=True)
    var = jnp.mean(jnp.square(yf - mu), axis=-1, keepdims=True)
    yn = (yf - mu) * lax.rsqrt(var + EPS) * lg_ref[...] + lb_ref[...]
    out = _dot(_silu(yn).astype(BF16), pw_ref[...].astype(BF16))
    u_ref[...] = (out * _silu(gate_ref[...])).astype(u_ref.dtype)


def _state_spec(sl, nbb, rows):
    if sl is None:
        return pl.BlockSpec((nbb, rows, BRANCH_W), lambda b, i: (b, 0, 0))
    return pl.BlockSpec((None, nbb, rows, BRANCH_W), lambda b, i: (sl, b, 0, 0))


def _b_mixer(zm, zg, state, cw, cb, lg, lb, pw, l, sl, nbatch, t, tb, nbb, out_dtype):
    nt = t // tb
    assert nbb == 1 or nt == 1
    hist = B_CONV - 1
    rows = nbb * tb

    def vec(arr_rows):
        return pl.BlockSpec((None, arr_rows, BRANCH_W), lambda b, i: (l, 0, 0))

    return pl.pallas_call(
        functools.partial(_b_kernel, tb, nbb),
        grid=(nbatch // nbb, nt),
        in_specs=[
            _window_spec(rows, BRANCH_W, lambda b, i: ((b * nt + i) * rows, OFF_BA)),
            _window_spec(rows, BRANCH_W, lambda b, i: ((b * nt + i) * rows, OFF_BB)),
            _state_spec(sl, nbb, hist),
            vec(B_CONV), vec(1), vec(1), vec(1), vec(BRANCH_W),
            pl.BlockSpec((rows, BRANCH_W), lambda b, i: (b * nt + i, 1)),
        ],
        out_specs=[pl.BlockSpec((rows, BRANCH_W), lambda b, i: (b * nt + i, 0)),
                   pl.BlockSpec((nbb, hist, BRANCH_W), lambda b, i: (b, 0, 0))],
        out_shape=[jax.ShapeDtypeStruct((nbatch * t, BRANCH_W), out_dtype),
                   jax.ShapeDtypeStruct((nbatch, hist, BRANCH_W), F32)],
        scratch_shapes=[pltpu.VMEM((B_PAD + tb, BRANCH_W), F32), pltpu.VMEM((rows, BRANCH_W), F32),
                        pltpu.VMEM((SUBLANES - 1, tb + B_PAD - SUBLANES, BRANCH_W), F32),
                        pltpu.VMEM((B_CONV, SUBLANES, BRANCH_W), F32)],
        compiler_params=_cparams(("arbitrary", "arbitrary")),
        name="b_conv",
    )(zm, zm, state, cw, cb, lg, lb, pw, zg)


C_PAD = 16


def _c_kernel(tb, nbb, pos0, u_ref, st_ref, w_ref, sc_ref, gate_ref, o_ref, sto_ref, ubuf, s2, s4, s8, dbuf):
    i = pl.program_id(1)
    pos1 = (pos0 + i * tb + lax.broadcasted_iota(jnp.int32, (tb, 1), 0) + 1).astype(F32)
    grp = lax.broadcasted_iota(jnp.int32, (1, BRANCH_W), 1) // C_GROUP
    n2, n4, n8 = tb + 14, tb + 12, tb + 8
    for bi in range(nbb):
        seq = slice(bi * tb, (bi + 1) * tb)

        @pl.when(i == 0)
        def _():
            ubuf[1:C_PAD, :] = st_ref[bi]

        u = u_ref[seq, :]
        ubuf[C_PAD:C_PAD + tb, :] = u
        s2[0:n2, :] = ubuf[1:1 + n2, :] + ubuf[2:2 + n2, :]
        s4[0:n4, :] = s2[0:n4, :] + s2[2:2 + n4, :]
        s8[0:n8, :] = s4[0:n8, :] + s4[4:4 + n8, :]
        sums = (s2[14:14 + tb, :], s4[12:12 + tb, :], s8[8:8 + tb, :], s8[0:tb, :] + s8[8:8 + tb, :])
        mean = sums[3] / jnp.minimum(pos1, float(C_WINDOWS[3]))
        for gi in (2, 1, 0):
            mean = jnp.where(grp == gi, sums[gi] / jnp.minimum(pos1, float(C_WINDOWS[gi])), mean)
        dbuf[seq, :] = mean - u
        tail = ubuf[1 + tb:C_PAD + tb, :]
        sto_ref[bi] = tail
        ubuf[1:C_PAD, :] = tail
    y = _dot(dbuf[...].astype(BF16), w_ref[...].astype(BF16)) * sc_ref[...]
    o_ref[...] = (y * _silu(gate_ref[...])).astype(o_ref.dtype)


def _c_mixer(zm, zg, state, wbd, scale, l, sl, nbatch, t, tb, nbb, pos0, out_dtype):
    nt = t // tb
    assert nbb == 1 or nt == 1
    rows = nbb * tb
    return pl.pallas_call(
        functools.partial(_c_kernel, tb, nbb, pos0),
        grid=(nbatch // nbb, nt),
        in_specs=[
            _window_spec(rows, BRANCH_W, lambda b, i: ((b * nt + i) * rows, OFF_CU)),
            _state_spec(sl, nbb, C_STATE),
            pl.BlockSpec((None, BRANCH_W, BRANCH_W), lambda b, i: (l, 0, 0)),
            pl.BlockSpec((None, 1, BRANCH_W), lambda b, i: (l, 0, 0)),
            pl.BlockSpec((rows, BRANCH_W), lambda b, i: (b * nt + i, 2)),
        ],
        out_specs=[pl.BlockSpec((rows, BRANCH_W), lambda b, i: (b * nt + i, 0)),
                   pl.BlockSpec((nbb, C_STATE, BRANCH_W), lambda b, i: (b, 0, 0))],
        out_shape=[jax.ShapeDtypeStruct((nbatch * t, BRANCH_W), out_dtype),
                   jax.ShapeDtypeStruct((nbatch, C_STATE, BRANCH_W), F32)],
        scratch_shapes=[pltpu.VMEM((C_PAD + tb, BRANCH_W), F32)] * 4 + [pltpu.VMEM((rows, BRANCH_W), F32)],
        compiler_params=_cparams(("arbitrary", "arbitrary")),
        name="c_pool",
    )(zm, state, wbd, scale, zg)


D_COMBINE_ROWS = 512
D_BLOCK_UNROLL = 4


def _d_prompt_kernel(q_ref, k_ref, v_ref, gate_ref, u_ref, o_s, l_s):
    lo = _lane_lo()
    rp = lax.broadcasted_iota(jnp.int32, (2 * BLOCK, BLOCK), 0) & (BLOCK - 1)
    jp = lax.broadcasted_iota(jnp.int32, (2 * BLOCK, BLOCK), 1)
    valid_c = jp <= rp
    valid_p = jp >= rp
    for p, (_, d) in enumerate(D_PATTERNS):
        nb = SEQ // d // BLOCK
        span = BLOCK * d

        def rows(start):
            return pl.ds(start, BLOCK) if d == 1 else pl.ds(start, BLOCK, stride=d)

        def block(it, carry):
            kp, vp = carry
            r = it // nb
            i = it - r * nb
            cur = r + i * span
            prev_bias = jnp.where(i > 0, 0.0, NEG)
            qc = q_ref[rows(cur), :] * ATTN_SCALE
            qs = jnp.concatenate([jnp.where(lo, qc, 0.0), jnp.where(lo, 0.0, qc)], axis=0).astype(BF16)
            kc = k_ref[rows(cur), :].astype(BF16)
            vc = v_ref[rows(cur), :].astype(BF16)
            o, lse = _softmax_attn(qs, kp, kc, vp, vc, valid_p, valid_c, None, prev_bias)
            o_s[p, rows(cur), :] = jnp.where(lo, o[0:BLOCK], o[BLOCK:2 * BLOCK])
            l_s[p, rows(cur), :] = jnp.where(lo, lse[0:BLOCK], lse[BLOCK:2 * BLOCK])
            return kc, vc

        zero = jnp.zeros((BLOCK, LANES), BF16)
        lax.fori_loop(0, d * nb, block, (zero, zero), unroll=D_BLOCK_UNROLL)

    def combine(t, carry):
        sl = pl.ds(pl.multiple_of(t * D_COMBINE_ROWS, D_COMBINE_ROWS), D_COMBINE_ROWS)
        la, lb, lc = l_s[0, sl, :], l_s[1, sl, :], l_s[2, sl, :]
        m = jnp.maximum(jnp.maximum(la, lb), lc)
        ea, eb, ec = jnp.exp(la - m), jnp.exp(lb - m), jnp.exp(lc - m)
        tot = ea + eb + ec
        y = (ea / tot) * o_s[0, sl, :] + (eb / tot) * o_s[1, sl, :] + (ec / tot) * o_s[2, sl, :]
        u_ref[sl, :] = (y * _silu(gate_ref[sl, :])).astype(u_ref.dtype)
        return carry

    lax.fori_loop(0, SEQ // D_COMBINE_ROWS, combine, 0)


def _d_prompt(dqn, dkn, zm, zg, nbatch):
    nchunk = BRANCH_W // LANES
    gate0 = 3 * nchunk
    qk_spec = pl.BlockSpec((SEQ, LANES), lambda b, c: (b, c))
    return pl.pallas_call(
        _d_prompt_kernel,
        grid=(nbatch, nchunk),
        in_specs=[
            qk_spec, qk_spec,
            _window_spec(SEQ, LANES, lambda b, c: (b * SEQ, pl.multiple_of(OFF_DV + c * LANES, LANES))),
            pl.BlockSpec((SEQ, LANES), lambda b, c: (b, gate0 + c)),
        ],
        out_specs=qk_spec,
        out_shape=jax.ShapeDtypeStruct((nbatch * SEQ, BRANCH_W), BF16),
        scratch_shapes=[pltpu.VMEM((len(D_PATTERNS), SEQ, LANES), F32)] * 2,
        compiler_params=_cparams(("parallel", "parallel")),
        name="d_prompt",
    )(dqn, dkn, zm, zg)


D_GROUP = 4
D_GROUP_W = D_GROUP * HEAD_DIM


def _d_sample_kernel(q_ref, kn_ref, vn_ref, gate_ref, cache_ref, *rest):
    u_ref, out_ref = rest[-2], rest[-1]
    t = q_ref.shape[0]
    hist = cache_ref.shape[3]
    nrow = D_GROUP * t
    q = q_ref[...]
    kn, vn = kn_ref[...], vn_ref[...]

    row_head = lax.broadcasted_iota(jnp.int32, (nrow, D_GROUP_W), 0) // t
    lane_head = lax.broadcasted_iota(jnp.int32, (nrow, D_GROUP_W), 1) // HEAD_DIM
    own = row_head == lane_head
    qbd = jnp.where(own, jnp.concatenate([q * ATTN_SCALE] * D_GROUP, axis=0), 0.0).astype(BF16)

    k_t = cache_ref[0].reshape(D_GROUP_W, hist)
    v_t = cache_ref[1].reshape(D_GROUP_W, hist)
    knp = _pad_rows(kn, BLOCK)
    vnp = _pad_rows(vn, BLOCK)
    s_c = _dot(qbd, k_t.astype(BF16))
    s_n = _dot_nt(qbd, knp.astype(BF16))

    trow_c = lax.broadcasted_iota(jnp.int32, (nrow, hist), 0) & (t - 1)
    delta_c = hist + trow_c - lax.broadcasted_iota(jnp.int32, (nrow, hist), 1)
    trow_n = lax.broadcasted_iota(jnp.int32, (nrow, BLOCK), 0) & (t - 1)
    delta_n = trow_n - lax.broadcasted_iota(jnp.int32, (nrow, BLOCK), 1)
    ecs, ens, dens, lses = [], [], [], []
    for w, d in D_PATTERNS:
        ok_c = ((delta_c & (d - 1)) == 0) & (delta_c <= w)
        ok_n = (delta_n >= 0) & ((delta_n & (d - 1)) == 0)
        sc = jnp.where(ok_c, s_c, NEG)
        sn = jnp.where(ok_n, s_n, NEG)
        m = jnp.maximum(jnp.max(sc, axis=-1, keepdims=True), jnp.max(sn, axis=-1, keepdims=True))
        ec = jnp.exp(sc - m)
        en = jnp.exp(sn - m)
        den = jnp.sum(ec, axis=-1, keepdims=True) + jnp.sum(en, axis=-1, keepdims=True)
        ecs.append(ec)
        ens.append(en)
        dens.append(den)
        lses.append(m + jnp.log(den))
    mm = jnp.maximum(jnp.maximum(lses[0], lses[1]), lses[2])
    wts = [jnp.exp(x - mm) for x in lses]
    tot = wts[0] + wts[1] + wts[2]
    coef = [wts[p] / tot / dens[p] for p in range(3)]
    pc = coef[0] * ecs[0] + coef[1] * ecs[1] + coef[2] * ecs[2]
    pn = coef[0] * ens[0] + coef[1] * ens[1] + coef[2] * ens[2]
    yfull = _dot_nt(pc.astype(BF16), v_t.astype(BF16)) + _dot(pn.astype(BF16), vnp.astype(BF16))
    y = jnp.zeros((t, D_GROUP_W), F32)
    for j in range(D_GROUP):
        y = y + jnp.where(own[j * t:(j + 1) * t], yfull[j * t:(j + 1) * t], 0.0)
    u_ref[...] = (y * _silu(gate_ref[...])).astype(u_ref.dtype)

    lane = lax.broadcasted_iota(jnp.int32, (1, LANES), 1)
    is_new = lane >= LANES - t
    for kv, new in ((0, knp), (1, vnp)):
        new_t = pltpu.roll(new.T, LANES - t, 1)
        old = cache_ref[kv].reshape(D_GROUP_W, hist)
        shifted = pltpu.roll(old, hist - t, 1)
        out_ref[kv, :, :, 0:hist - LANES] = shifted[:, 0:hist - LANES].reshape(D_GROUP, HEAD_DIM, hist - LANES)
        tail = jnp.where(is_new, new_t, shifted[:, hist - LANES:hist])
        out_ref[kv, :, :, hist - LANES:hist] = tail.reshape(D_GROUP, HEAD_DIM, LANES)


def _d_sample(dqn, dkn, zm, zg, cache_t, prev_out, l, nbatch, t):
    ngroup = cache_t.shape[3] // D_GROUP
    gate0 = 3 * BRANCH_W // D_GROUP_W
    cache_spec = pl.BlockSpec((None, None, 2, D_GROUP, HEAD_DIM, cache_t.shape[5]),
                              lambda b, g: (l, b, 0, g, 0, 0))
    small = pl.BlockSpec((t, D_GROUP_W), lambda b, g: (b, g))
    in_specs = [small, small,
                _window_spec(t, D_GROUP_W, lambda b, g: (b * t, pl.multiple_of(OFF_DV + g * D_GROUP_W, LANES))),
                pl.BlockSpec((t, D_GROUP_W), lambda b, g: (b, gate0 + g)),
                cache_spec]
    args = [dqn, dkn, zm, zg, cache_t]
    aliases = {}
    if prev_out is not None:
        in_specs.append(pl.BlockSpec(memory_space=pl.ANY))
        args.append(prev_out)
        aliases = {len(args) - 1: 1}
    return pl.pallas_call(
        _d_sample_kernel,
        grid=(nbatch, ngroup),
        in_specs=in_specs,
        out_specs=[small, cache_spec],
        out_shape=[jax.ShapeDtypeStruct((nbatch * t, BRANCH_W), F32),
                   jax.ShapeDtypeStruct(cache_t.shape, F32)],
        input_output_aliases=aliases,
        compiler_params=_cparams(("parallel", "parallel")),
        name="d_sample",
    )(*args)


M_WIN = 2 * LANES
M_WIN_START = (0, 128, 384, 512)


def _m_head_mask(h):
    lane = lax.broadcasted_iota(jnp.int32, (1, M_WIN), 1)
    first = h * M_HEAD_DIM - M_WIN_START[h]
    return (lane >= first) & (lane < first + M_HEAD_DIM)


def _mem_kv_kernel(x_ref, g_ref, w_ref, kg_ref, ones_ref, o_ref):
    x = x_ref[...]
    ms = jnp.mean(x * x, axis=-1, keepdims=True)
    xn = (x * lax.rsqrt(ms + EPS) * g_ref[...]).astype(BF16)
    kv = _dot(xn, w_ref[...].astype(BF16))
    k = kv[:, 0:BRANCH_W]
    kms = _seg_sum(k * k, ones_ref[...]) * (1.0 / M_HEAD_DIM)
    o_ref[:, 0:BRANCH_W] = k * lax.rsqrt(kms + EPS) * kg_ref[...]
    o_ref[:, BRANCH_W:2 * BRANCH_W] = kv[:, BRANCH_W:2 * BRANCH_W]


def _mem_kv(mem, g, w, kg, ones_bd, l):
    nbatch = mem.shape[0]
    return pl.pallas_call(
        _mem_kv_kernel,
        grid=(nbatch,),
        in_specs=[
            pl.BlockSpec((None, N_MEM, D_MODEL), lambda b: (b, 0, 0)),
            pl.BlockSpec((None, 1, D_MODEL), lambda b: (l, 0, 0)),
            pl.BlockSpec((None, D_MODEL, 2 * BRANCH_W), lambda b: (l, 0, 0)),
            pl.BlockSpec((None, 1, BRANCH_W), lambda b: (l, 0, 0)),
            pl.BlockSpec((BRANCH_W, BRANCH_W), lambda b: (0, 0)),
        ],
        out_specs=pl.BlockSpec((None, N_MEM, 2 * BRANCH_W), lambda b: (b, 0, 0)),
        out_shape=jax.ShapeDtypeStruct((nbatch, N_MEM, 2 * BRANCH_W), F32),
        compiler_params=_cparams(("parallel",)),
        name="mem_kv",
    )(mem, g, w, kg, ones_bd)


def _m_query(q, qg, ones_bd):
    ms = _seg_sum(q * q, ones_bd) * (1.0 / M_HEAD_DIM)
    return (q * lax.rsqrt(ms + EPS) * qg) * (M_HEAD_DIM ** -0.5)


def _m_softmax(s):
    e = jnp.exp(s - jnp.max(s, axis=-1, keepdims=True))
    return (e / jnp.sum(e, axis=-1, keepdims=True)).astype(BF16)


def _m_assemble(outs):
    lo = _lane_lo()
    return jnp.concatenate([
        outs[0][:, 0:LANES],
        jnp.where(lo, outs[0][:, LANES:M_WIN], outs[1][:, 0:LANES]),
        outs[1][:, LANES:M_WIN],
        outs[2][:, 0:LANES],
        jnp.where(lo, outs[2][:, LANES:M_WIN], outs[3][:, 0:LANES]),
        outs[3][:, LANES:M_WIN],
    ], axis=1)


def _m_attn_kernel(q_ref, kv_ref, gate_ref, qg_ref, ones_ref, u_ref):
    qn = _m_query(q_ref[...], qg_ref[...], ones_ref[...])
    outs = []
    for h in range(M_HEADS):
        win = slice(M_WIN_START[h], M_WIN_START[h] + M_WIN)
        qh = jnp.where(_m_head_mask(h), qn[:, win], 0.0).astype(BF16)
        p = _m_softmax(_dot_nt(qh, kv_ref[:, win].astype(BF16)))
        vwin = slice(BRANCH_W + M_WIN_START[h], BRANCH_W + M_WIN_START[h] + M_WIN)
        outs.append(_dot(p, kv_ref[:, vwin].astype(BF16)))
    u_ref[...] = (_m_assemble(outs) * _silu(gate_ref[...])).astype(u_ref.dtype)


def _m_sample_kernel(nbb, t, q_ref, kv_ref, gate_ref, qg_ref, ones_ref, u_ref):
    qn = _m_query(q_ref[...], qg_ref[...], ones_ref[...])
    ys = []
    for bi in range(nbb):
        k_t = kv_ref[bi, 0].reshape(BRANCH_W, N_MEM)
        v_t = kv_ref[bi, 1].reshape(BRANCH_W, N_MEM)
        qb = qn[bi * t:(bi + 1) * t]
        outs = []
        for h in range(M_HEADS):
            win = slice(M_WIN_START[h], M_WIN_START[h] + M_WIN)
            qh = jnp.where(_m_head_mask(h), qb[:, win], 0.0).astype(BF16)
            p = _m_softmax(_dot(qh, k_t[win].astype(BF16)))
            outs.append(_dot_nt(p, v_t[win].astype(BF16)))
        ys.append(_m_assemble(outs))
    y = jnp.concatenate(ys, axis=0)
    u_ref[...] = (y * _silu(gate_ref[...])).astype(u_ref.dtype)


def _m_sample(zm, zg, cache_t, qg, ones_bd, l, nbatch, t, nbb):
    rows = nbb * t
    return pl.pallas_call(
        functools.partial(_m_sample_kernel, nbb, t),
        grid=(nbatch // nbb,),
        in_specs=[
            _window_spec(rows, BRANCH_W, lambda i: (i * rows, OFF_MQ)),
            pl.BlockSpec((None, nbb, 2, M_HEADS, M_HEAD_DIM, N_MEM), lambda i: (l, i, 0, 0, 0, 0)),
            pl.BlockSpec((rows, BRANCH_W), lambda i: (i, 4)),
            pl.BlockSpec((None, 1, BRANCH_W), lambda i: (l, 0, 0)),
            pl.BlockSpec((BRANCH_W, BRANCH_W), lambda i: (0, 0)),
        ],
        out_specs=pl.BlockSpec((rows, BRANCH_W), lambda i: (i, 0)),
        out_shape=jax.ShapeDtypeStruct((nbatch * t, BRANCH_W), F32),
        compiler_params=_cparams(("parallel",)),
        name="m_sample",
    )(zm, cache_t, zg, qg, ones_bd)


def _m_attn(zm, zg, mkv, qg, ones_bd, l, nbatch, t, tq, out_dtype):
    nt = t // tq
    kv_spec = pl.BlockSpec((None, N_MEM, 2 * BRANCH_W), lambda b, i: (b, 0, 0))
    return pl.pallas_call(
        _m_attn_kernel,
        grid=(nbatch, nt),
        in_specs=[
            _window_spec(tq, BRANCH_W, lambda b, i: ((b * nt + i) * tq, OFF_MQ)),
            kv_spec,
            pl.BlockSpec((tq, BRANCH_W), lambda b, i: (b * nt + i, 4)),
            pl.BlockSpec((None, 1, BRANCH_W), lambda b, i: (l, 0, 0)),
            pl.BlockSpec((BRANCH_W, BRANCH_W), lambda b, i: (0, 0)),
        ],
        out_specs=pl.BlockSpec((tq, BRANCH_W), lambda b, i: (b * nt + i, 0)),
        out_shape=jax.ShapeDtypeStruct((nbatch * t, BRANCH_W), out_dtype),
        compiler_params=_cparams(("parallel", "parallel")),
        name="m_attn",
    )(zm, mkv, zg, qg, ones_bd)


def _out_proj_kernel(ua, ub, uc, ud, um, w_ref, h_ref, o_ref):
    acc = h_ref[...]
    for g, u_ref in enumerate((ua, ub, uc, ud, um)):
        wg = w_ref[g * BRANCH_W:(g + 1) * BRANCH_W, :].astype(BF16)
        acc = acc + _dot(u_ref[...].astype(BF16), wg)
    o_ref[...] = acc


def _out_proj(us, w, h, l, tm, tn):
    m = h.shape[0]
    u_spec = pl.BlockSpec((tm, BRANCH_W), lambda i, j: (i, 0), pipeline_mode=pl.Buffered(1))
    return pl.pallas_call(
        _out_proj_kernel,
        grid=(m // tm, D_MODEL // tn),
        in_specs=[u_spec] * N_BRANCH + [
            pl.BlockSpec((None, MIX_W, tn), lambda i, j: (l, 0, j)),
            pl.BlockSpec((tm, tn), lambda i, j: (i, j)),
        ],
        out_specs=pl.BlockSpec((tm, tn), lambda i, j: (i, j)),
        out_shape=jax.ShapeDtypeStruct((m, D_MODEL), F32),
        compiler_params=_cparams(("parallel", "parallel")),
        name="out_proj",
    )(*us, w, h)


def _rope_tables(pos):
    half = ROT_DIM // 2
    inv = ROPE_THETA ** (-jnp.arange(half, dtype=F32) / half)
    ang = pos.astype(F32)[:, None] * inv[None, :]
    cos, sin = jnp.cos(ang), jnp.sin(ang)
    n = pos.shape[0]
    rest = HEAD_DIM - ROT_DIM
    one_head = lambda a, b, fill: jnp.concatenate([a, b, jnp.full((n, rest), fill, F32)], axis=1)
    zero = jnp.zeros_like(sin)
    cos_t = one_head(cos, cos, 1.0)
    slo_t = one_head(-sin, zero, 0.0)
    shi_t = one_head(zero, sin, 0.0)
    return tuple(jnp.concatenate([a, a], axis=1) for a in (cos_t, slo_t, shi_t))


def _block_ones(width, seg):
    idx = jnp.arange(width) // seg
    return (idx[:, None] == idx[None, :]).astype(BF16)


def _tile_lanes(x, width):
    return jnp.tile(x, (1, width // x.shape[-1]))


def kernel(x_prompt, x_sample, cache_a_kv, state_b_conv, state_c_pool, cache_d_kv, cache_mem_kv, mem_prompt,
           norm_g, w_in, a_q_norm, a_k_norm, a_sinks, b_conv_w, b_conv_b, b_ln_g, b_ln_b, b_w_pw, c_w_group,
           c_scale, d_q_norm, d_k_norm, m_norm_g, m_w_kv, m_q_norm, m_k_norm, w_out):
    bp, seq = x_prompt.shape[0], x_prompt.shape[1]
    bs, ts = x_sample.shape[0], x_sample.shape[1]
    depth = w_in.shape[0]
    hist_d = cache_d_kv.shape[2]

    hp = x_prompt.reshape(bp * seq, D_MODEL)
    hs = x_sample.reshape(bs * ts, D_MODEL)

    tabs_p = _rope_tables(jnp.arange(seq, dtype=jnp.int32))
    tabs_s = tuple(jnp.tile(a, (bs, 1)) for a in _rope_tables(PAST_LEN + jnp.arange(ts, dtype=jnp.int32)))
    ones64 = _block_ones(LANES, HEAD_DIM)
    ones192 = _block_ones(BRANCH_W, M_HEAD_DIM)

    row = lambda a: a.reshape(depth, 1, a.shape[-1])
    norm_g3, m_norm_g3 = row(norm_g), row(m_norm_g)
    qk_gains = jnp.stack([_tile_lanes(g, LANES) for g in (a_q_norm, a_k_norm, d_q_norm, d_k_norm)], axis=1)
    m_qg, m_kg = row(_tile_lanes(m_q_norm, BRANCH_W)), row(_tile_lanes(m_k_norm, BRANCH_W))
    b_cb, b_lg, b_lb, c_sc = row(b_conv_b), row(b_ln_g), row(b_ln_b), row(c_scale)
    c_wbd = jnp.zeros((depth, BRANCH_W, BRANCH_W), F32)
    for gi in range(len(C_WINDOWS)):
        sl = slice(gi * C_GROUP, (gi + 1) * C_GROUP)
        c_wbd = c_wbd.at[:, sl, sl].set(c_w_group[:, gi])

    cache_a = cache_a_kv.reshape(depth, bs, A_WINDOW, 2 * A_KV_W)
    cache_d_t = jnp.transpose(cache_d_kv, (0, 1, 3, 4, 5, 2))
    sd_t = None
    cache_m_t = jnp.transpose(cache_mem_kv, (0, 1, 3, 4, 5, 2))
    zero_b = jnp.zeros((bp, B_CONV - 1, BRANCH_W), F32)
    zero_c = jnp.zeros((bp, C_STATE, BRANCH_W), F32)

    pa, pb, pc, pd, pm, sa, sb, sc = ([] for _ in range(8))
    for l in range(depth):
        mkv = _mem_kv(mem_prompt, m_norm_g3, m_w_kv, m_kg, ones192, l)
        pm.append(mkv)

        xn = _rms_norm(hp, norm_g3, l, 512)
        zm = _in_proj(xn, w_in, l, 0, MIXIN_W, 512, PROJ_ROWS)
        zg = _in_proj(xn, w_in, l, OFF_GATE, MIX_W, 768, PROJ_ROWS)
        aqn, akn, dqn, dkn = _prep(zm, tabs_p, qk_gains, ones64, l, 512)
        ua, a_st = _a_prompt(aqn, akn, zm, zg, a_sinks, l, bp)
        ub, b_st = _b_mixer(zm, zg, zero_b, b_conv_w, b_cb, b_lg, b_lb, b_w_pw, l, None, bp, seq, 512, 1, BF16)
        uc, c_st = _c_mixer(zm, zg, zero_c, c_wbd, c_sc, l, None, bp, seq, 512, 1, 0, BF16)
        ud = _d_prompt(dqn, dkn, zm, zg, bp)
        um = _m_attn(zm, zg, mkv, m_qg, ones192, l, bp, seq, 512, BF16)
        hp = _out_proj((ua, ub, uc, ud, um), w_out, hp, l, PROJ_ROWS, 256)
        pa.append(a_st)
        pb.append(b_st)
        pc.append(c_st)
        keep = min(D_WINDOW, seq)
        pd.append(jnp.concatenate(
            [dkn.reshape(bp, seq, BRANCH_W)[:, seq - keep:],
             zm.reshape(bp, seq, MIXIN_W)[:, seq - keep:, OFF_DV:OFF_DV + BRANCH_W]], axis=-1))

        rows_s = bs * ts
        xn = _rms_norm(hs, norm_g3, l, rows_s)
        zm = _in_proj(xn, w_in, l, 0, MIXIN_W, 512, rows_s)
        zg = _in_proj(xn, w_in, l, OFF_GATE, MIX_W, 768, rows_s)
        aqn, akn, dqn, dkn = _prep(zm, tabs_s, qk_gains, ones64, l, rows_s)
        ua, a_st = _a_sample(aqn, akn, zm, zg, a_sinks, cache_a, l, bs, ts, SAMPLE_GROUP)
        ub, b_st = _b_mixer(zm, zg, state_b_conv, b_conv_w, b_cb, b_lg, b_lb, b_w_pw, l, l, bs, ts, ts,
                            SAMPLE_GROUP, F32)
        uc, c_st = _c_mixer(zm, zg, state_c_pool, c_wbd, c_sc, l, l, bs, ts, ts, SAMPLE_GROUP, PAST_LEN, F32)
        ud, sd_t = _d_sample(dqn, dkn, zm, zg, cache_d_t, sd_t, l, bs, ts)
        um = _m_sample(zm, zg, cache_m_t, m_qg, ones192, l, bs, ts, SAMPLE_GROUP)
        hs = _out_proj((ua, ub, uc, ud, um), w_out, hs, l, rows_s, 512)
        sa.append(a_st)
        sb.append(b_st)
        sc.append(c_st)

    kv5 = lambda a, heads, hd: a.reshape(a.shape[:-1] + (2, heads, hd))
    s_d_kv = jnp.transpose(sd_t, (0, 1, 5, 2, 3, 4))
    return (hp.reshape(bp, seq, D_MODEL), hs.reshape(bs, ts, D_MODEL),
            kv5(jnp.stack(pa), A_KV_HEADS, HEAD_DIM), jnp.stack(pb), jnp.stack(pc),
            kv5(jnp.stack(pd), BRANCH_W // HEAD_DIM, HEAD_DIM), kv5(jnp.stack(pm), M_HEADS, M_HEAD_DIM),
            kv5(jnp.stack(sa), A_KV_HEADS, HEAD_DIM), jnp.stack(sb), jnp.stack(sc), s_d_kv)
```

```python
import functools

import jax
import jax.numpy as jnp
from jax import lax
from jax.experimental import pallas as pl
from jax.experimental.pallas import tpu as pltpu

F32 = jnp.float32
BF16 = jnp.bfloat16

D_MODEL = 2048
SEQ = 4096
DEPTH = 4
PAST_LEN = 16384
HEAD_DIM = 64
BRANCH_W = 768
N_BRANCH = 5
MIX_W = N_BRANCH * BRANCH_W
A_HEADS = 12
A_KV_HEADS = 4
A_KV_W = A_KV_HEADS * HEAD_DIM
A_WINDOW = 128
B_CONV = 31
C_WINDOWS = (2, 4, 8, 16)
C_GROUP = BRANCH_W // 4
C_STATE = 15
D_PATTERNS = ((128, 1), (512, 4), (2048, 16))
D_WINDOW = 2048
M_HEADS = 4
M_HEAD_DIM = 192
N_MEM = 256
ROT_DIM = 16
ROPE_THETA = 500000.0
BLOCK = 128
EPS = 1e-6
NEG = -1e30
ATTN_SCALE = HEAD_DIM ** -0.5

OFF_AQ, OFF_AK, OFF_AV, OFF_BA, OFF_BB, OFF_CU, OFF_DQ, OFF_DK, OFF_DV, OFF_MQ, OFF_GATE = (
    0, 768, 1024, 1280, 2048, 2816, 3584, 4352, 5120, 5888, 6656)
MIXIN_W = OFF_GATE
IN_W = MIXIN_W + MIX_W

LANES = 128
SUBLANES = 8
SAMPLE_GROUP = 4
PROJ_ROWS = 2048
V7X_VMEM_LIMIT = 56 * 1024 * 1024


def _cparams(sem):
    return pltpu.CompilerParams(dimension_semantics=sem, vmem_limit_bytes=V7X_VMEM_LIMIT)


def _dot(a, b):
    return jnp.dot(a, b, preferred_element_type=F32)


def _dot_nt(a, b):
    return lax.dot_general(a, b, (((1,), (1,)), ((), ())), preferred_element_type=F32)


def _silu(x):
    return x * jax.nn.sigmoid(x)


def _window_spec(rows, cols, origin):
    return pl.BlockSpec((pl.Element(rows), pl.Element(cols)), origin)


def _seg_sum(x, ones_bd):
    hi = x.astype(BF16)
    lo = (x - hi.astype(F32)).astype(BF16)
    return _dot(hi, ones_bd) + _dot(lo, ones_bd)


def _rms_norm_kernel(x_ref, g_ref, o_ref):
    x = x_ref[...]
    ms = jnp.mean(x * x, axis=-1, keepdims=True)
    o_ref[...] = (x * lax.rsqrt(ms + EPS) * g_ref[...]).astype(o_ref.dtype)


def _rms_norm(x, g, l, tm):
    m = x.shape[0]
    return pl.pallas_call(
        _rms_norm_kernel,
        grid=(m // tm,),
        in_specs=[pl.BlockSpec((tm, D_MODEL), lambda i: (i, 0)),
                  pl.BlockSpec((None, 1, D_MODEL), lambda i: (l, 0, 0))],
        out_specs=pl.BlockSpec((tm, D_MODEL), lambda i: (i, 0)),
        out_shape=jax.ShapeDtypeStruct((m, D_MODEL), BF16),
        compiler_params=_cparams(("parallel",)),
        name="rms_norm",
    )(x, g)


def _in_proj_kernel(x_ref, w_ref, o_ref):
    o_ref[...] = _dot(x_ref[...], w_ref[...].astype(BF16))


def _in_proj(xn, w, l, col0, ncols, tn, tm):
    m = xn.shape[0]
    return pl.pallas_call(
        _in_proj_kernel,
        grid=(m // tm, ncols // tn),
        in_specs=[
            pl.BlockSpec((tm, D_MODEL), lambda i, j: (i, 0), pipeline_mode=pl.Buffered(1)),
            _window_spec(D_MODEL, tn, lambda i, j: (l * D_MODEL, pl.multiple_of(col0 + j * tn, LANES))),
        ],
        out_specs=pl.BlockSpec((tm, tn), lambda i, j: (i, j)),
        out_shape=jax.ShapeDtypeStruct((m, ncols), F32),
        compiler_params=_cparams(("parallel", "parallel")),
        name="in_proj",
    )(xn, w.reshape(w.shape[0] * D_MODEL, w.shape[2]))


def _norm_rope_chunk(x, gain, ones_bd, cos, sin_lo, sin_hi):
    ms = _seg_sum(x * x, ones_bd) * (1.0 / HEAD_DIM)
    y = x * lax.rsqrt(ms + EPS) * gain
    return y * cos + pltpu.roll(y, LANES - 8, 1) * sin_lo + pltpu.roll(y, 8, 1) * sin_hi


def _prep_kernel(aq_ref, ak_ref, dq_ref, dk_ref, cos_ref, slo_ref, shi_ref, gains_ref, ones_ref,
                 aqn_ref, akn_ref, dqn_ref, dkn_ref):
    cos, slo, shi = cos_ref[...], slo_ref[...], shi_ref[...]
    ones_bd = ones_ref[...]
    for idx, (src, dst) in enumerate(((aq_ref, aqn_ref), (ak_ref, akn_ref), (dq_ref, dqn_ref),
                                      (dk_ref, dkn_ref))):
        gain = gains_ref[idx:idx + 1, :]
        for c in range(src.shape[1] // LANES):
            cols = slice(c * LANES, (c + 1) * LANES)
            dst[:, cols] = _norm_rope_chunk(src[:, cols], gain, ones_bd, cos, slo, shi)


def _prep(zm, tabs, gains, ones_bd, l, tm):
    m = zm.shape[0]
    cos, slo, shi = tabs
    nt = cos.shape[0] // tm
    tab_spec = pl.BlockSpec((tm, LANES), lambda i: (i % nt, 0))

    def seg(off, w):
        return _window_spec(tm, w, lambda i: (i * tm, off))

    return pl.pallas_call(
        _prep_kernel,
        grid=(m // tm,),
        in_specs=[seg(OFF_AQ, BRANCH_W), seg(OFF_AK, A_KV_W), seg(OFF_DQ, BRANCH_W), seg(OFF_DK, BRANCH_W),
                  tab_spec, tab_spec, tab_spec,
                  pl.BlockSpec((None, 4, LANES), lambda i: (l, 0, 0)),
                  pl.BlockSpec((LANES, LANES), lambda i: (0, 0))],
        out_specs=[pl.BlockSpec((tm, BRANCH_W), lambda i: (i, 0)),
                   pl.BlockSpec((tm, A_KV_W), lambda i: (i, 0)),
                   pl.BlockSpec((tm, BRANCH_W), lambda i: (i, 0)),
                   pl.BlockSpec((tm, BRANCH_W), lambda i: (i, 0))],
        out_shape=[jax.ShapeDtypeStruct((m, BRANCH_W), F32), jax.ShapeDtypeStruct((m, A_KV_W), F32),
                   jax.ShapeDtypeStruct((m, BRANCH_W), F32), jax.ShapeDtypeStruct((m, BRANCH_W), F32)],
        compiler_params=_cparams(("parallel",)),
        name="qk_norm_rope",
    )(zm, zm, zm, zm, cos, slo, shi, gains, ones_bd)


def _softmax_attn(qs, kp, kc, vp, vc, valid_p, valid_c, sink, prev_bias=None):
    sp = _dot_nt(qs, kp)
    if prev_bias is not None:
        sp = sp + prev_bias
    sp = jnp.where(valid_p, sp, NEG)
    sc = jnp.where(valid_c, _dot_nt(qs, kc), NEG)
    m = jnp.max(jnp.maximum(sp, sc), axis=-1, keepdims=True)
    if sink is not None:
        m = jnp.maximum(m, sink)
    ep = jnp.exp(sp - m)
    ec = jnp.exp(sc - m)
    den = jnp.sum(ep + ec, axis=-1, keepdims=True)
    if sink is not None:
        den = den + jnp.exp(sink - m)
    o = (_dot(ep.astype(BF16), vp) + _dot(ec.astype(BF16), vc)) / den
    return o, m + jnp.log(den)


def _band_masks(rows, tq, tk, prev_thr):
    rp = lax.broadcasted_iota(jnp.int32, (rows, BLOCK), 0) & (tq - 1)
    jp = lax.broadcasted_iota(jnp.int32, (rows, BLOCK), 1)
    rc = lax.broadcasted_iota(jnp.int32, (rows, tk), 0) & (tq - 1)
    jc = lax.broadcasted_iota(jnp.int32, (rows, tk), 1)
    return jp >= rp + prev_thr, jc <= rc


def _lane_lo():
    return lax.broadcasted_iota(jnp.int32, (1, LANES), 1) < HEAD_DIM


def _a_core(q, kp, kc, vp, vc, prev_thr, sink_ref, l):
    tq, tk = q.shape[0], kc.shape[0]
    group = A_HEADS // A_KV_HEADS
    lo = _lane_lo()
    valid_p, valid_c = _band_masks(group * tq, tq, tk, prev_thr)
    heads = [None] * A_HEADS
    for h in range(A_KV_HEADS):
        kcols = slice((h // 2) * LANES, (h // 2 + 1) * LANES)
        half = h % 2
        keep = lo if half == 0 else jnp.logical_not(lo)
        parts, sinks = [], []
        for g in range(group):
            j = group * h + g
            qc = q[:, (j // 2) * LANES:(j // 2 + 1) * LANES]
            if j % 2 != half:
                qc = pltpu.roll(qc, HEAD_DIM, 1)
            parts.append(jnp.where(keep, qc * ATTN_SCALE, 0.0))
            sinks.append(jnp.full((tq, 1), sink_ref[l, j], F32))
        qs = jnp.concatenate(parts, axis=0).astype(BF16)
        o, _ = _softmax_attn(qs, kp[:, kcols].astype(BF16), kc[:, kcols].astype(BF16),
                             vp[:, kcols].astype(BF16), vc[:, kcols].astype(BF16),
                             valid_p, valid_c, jnp.concatenate(sinks, axis=0))
        for g in range(group):
            j = group * h + g
            oj = o[g * tq:(g + 1) * tq]
            if j % 2 != half:
                oj = pltpu.roll(oj, HEAD_DIM, 1)
            heads[j] = oj
    return jnp.concatenate([jnp.where(lo, heads[2 * c], heads[2 * c + 1]) for c in range(A_HEADS // 2)],
                           axis=1)


A_STEP_BLOCKS = 1


def _a_prompt_kernel(l, nsteps, sink_ref, q_ref, kp_ref, kc_ref, vp_ref, vc_ref, gate_ref, u_ref, st_ref):
    i = pl.program_id(1)
    prev_thr = jnp.where(i > 0, 0, 2 * BLOCK)
    for s in range(A_STEP_BLOCKS):
        rows = slice(s * BLOCK, (s + 1) * BLOCK)
        before = slice((s - 1) * BLOCK, s * BLOCK)
        kp = kp_ref[...] if s == 0 else kc_ref[before, :]
        vp = vp_ref[...] if s == 0 else vc_ref[before, :]
        y = _a_core(q_ref[rows, :], kp, kc_ref[rows, :], vp, vc_ref[rows, :], prev_thr if s == 0 else 0,
                    sink_ref, l)
        u_ref[rows, :] = (y * _silu(gate_ref[rows, :])).astype(u_ref.dtype)

    @pl.when(i == nsteps - 1)
    def _():
        last = slice((A_STEP_BLOCKS - 1) * BLOCK, A_STEP_BLOCKS * BLOCK)
        st_ref[:, 0:A_KV_W] = kc_ref[last, :]
        st_ref[:, A_KV_W:2 * A_KV_W] = vc_ref[last, :]


def _a_prompt(aqn, akn, zm, zg, sinks, l, nbatch):
    rows = A_STEP_BLOCKS * BLOCK
    nsteps = SEQ // rows
    nb = SEQ // BLOCK

    def cur(b, i):
        return b * nsteps + i

    def prev(b, i):
        return b * nb + jnp.maximum(i * A_STEP_BLOCKS - 1, 0)

    return pl.pallas_call(
        functools.partial(_a_prompt_kernel, l, nsteps),
        grid=(nbatch, nsteps),
        in_specs=[
            pl.BlockSpec(memory_space=pltpu.SMEM),
            pl.BlockSpec((rows, BRANCH_W), lambda b, i: (cur(b, i), 0)),
            pl.BlockSpec((BLOCK, A_KV_W), lambda b, i: (prev(b, i), 0)),
            pl.BlockSpec((rows, A_KV_W), lambda b, i: (cur(b, i), 0)),
            _window_spec(BLOCK, A_KV_W, lambda b, i: (prev(b, i) * BLOCK, OFF_AV)),
            _window_spec(rows, A_KV_W, lambda b, i: (cur(b, i) * rows, OFF_AV)),
            pl.BlockSpec((rows, BRANCH_W), lambda b, i: (cur(b, i), 0)),
        ],
        out_specs=[pl.BlockSpec((rows, BRANCH_W), lambda b, i: (cur(b, i), 0)),
                   pl.BlockSpec((None, A_WINDOW, 2 * A_KV_W), lambda b, i: (b, 0, 0))],
        out_shape=[jax.ShapeDtypeStruct((nbatch * SEQ, BRANCH_W), BF16),
                   jax.ShapeDtypeStruct((nbatch, A_WINDOW, 2 * A_KV_W), F32)],
        compiler_params=_cparams(("arbitrary", "arbitrary")),
        name="a_prompt",
    )(sinks, aqn, akn, akn, zm, zm, zg)


def _pad_rows(x, rows):
    return jnp.concatenate([x, jnp.zeros((rows - x.shape[0], x.shape[1]), x.dtype)], axis=0)


def _a_sample_kernel(l, nbb, t, sink_ref, q_ref, kn_ref, vn_ref, cache_ref, gate_ref, u_ref, st_ref):
    ys = []
    for bi in range(nbb):
        rows = slice(bi * t, (bi + 1) * t)
        kn, vn = kn_ref[rows, :], vn_ref[rows, :]
        kp = cache_ref[bi, :, 0:A_KV_W]
        vp = cache_ref[bi, :, A_KV_W:2 * A_KV_W]
        ys.append(_a_core(q_ref[rows, :], kp, _pad_rows(kn, BLOCK), vp, _pad_rows(vn, BLOCK), 0, sink_ref, l))
        st_ref[bi, 0:A_WINDOW - t, :] = cache_ref[bi, t:A_WINDOW, :]
        st_ref[bi, A_WINDOW - t:A_WINDOW, 0:A_KV_W] = kn
        st_ref[bi, A_WINDOW - t:A_WINDOW, A_KV_W:2 * A_KV_W] = vn
    y = jnp.concatenate(ys, axis=0)
    u_ref[...] = (y * _silu(gate_ref[...])).astype(u_ref.dtype)


def _a_sample(aqn, akn, zm, zg, sinks, cache, l, nbatch, t, nbb):
    rows = nbb * t
    return pl.pallas_call(
        functools.partial(_a_sample_kernel, l, nbb, t),
        grid=(nbatch // nbb,),
        in_specs=[
            pl.BlockSpec(memory_space=pltpu.SMEM),
            pl.BlockSpec((rows, BRANCH_W), lambda i: (i, 0)),
            pl.BlockSpec((rows, A_KV_W), lambda i: (i, 0)),
            _window_spec(rows, A_KV_W, lambda i: (i * rows, OFF_AV)),
            pl.BlockSpec((None, nbb, A_WINDOW, 2 * A_KV_W), lambda i: (l, i, 0, 0)),
            pl.BlockSpec((rows, BRANCH_W), lambda i: (i, 0)),
        ],
        out_specs=[pl.BlockSpec((rows, BRANCH_W), lambda i: (i, 0)),
                   pl.BlockSpec((nbb, A_WINDOW, 2 * A_KV_W), lambda i: (i, 0, 0))],
        out_shape=[jax.ShapeDtypeStruct((nbatch * t, BRANCH_W), F32),
                   jax.ShapeDtypeStruct((nbatch, A_WINDOW, 2 * A_KV_W), F32)],
        compiler_params=_cparams(("parallel",)),
        name="a_sample",
    )(sinks, aqn, akn, zm, cache, zg)


B_PAD = 32
B_CONV_ROWS = 32


def _b_kernel(tb, nbb, za_ref, zb_ref, st_ref, cw_ref, cb_ref, lg_ref, lb_ref, pw_ref, gate_ref,
              u_ref, sto_ref, gbuf, ybuf, sbuf, wtile):
    hist = B_CONV - 1
    first = B_PAD - hist
    rc = min(tb, B_CONV_ROWS)
    span = tb + B_PAD - SUBLANES
    for j in range(B_CONV):
        wtile[j] = jnp.broadcast_to(cw_ref[j:j + 1, :], (SUBLANES, BRANCH_W))
    for bi in range(nbb):
        seq = slice(bi * tb, (bi + 1) * tb)

        @pl.when(pl.program_id(1) == 0)
        def _():
            gbuf[first:B_PAD, :] = st_ref[bi]

        gbuf[B_PAD:B_PAD + tb, :] = za_ref[seq, :] * jax.nn.sigmoid(zb_ref[seq, :])
        for s in range(1, SUBLANES):
            sbuf[s - 1] = gbuf[s:s + span, :]
        for base in range(0, tb, rc):
            acc = jnp.zeros((rc // SUBLANES, SUBLANES, BRANCH_W), F32)
            for j in range(B_CONV):
                whole, s = divmod(first + j, SUBLANES)
                row0 = base + whole * SUBLANES
                rows = gbuf[row0:row0 + rc, :] if s == 0 else sbuf[s - 1, row0:row0 + rc, :]
                acc = acc + wtile[j][None] * rows.reshape(rc // SUBLANES, SUBLANES, BRANCH_W)
            ybuf[bi * tb + base:bi * tb + base + rc, :] = acc.reshape(rc, BRANCH_W)
        tail = gbuf[first + tb:B_PAD + tb, :]
        sto_ref[bi] = tail
        gbuf[first:B_PAD, :] = tail
    yf = ybuf[...] + cb_ref[...]
    mu = jnp.mean(yf, axis=-1, keepdims=True)
    var = jnp.mean(jnp.square(yf - mu), axis=-1, keepdims=True)
    yn = (yf - mu) * lax.rsqrt(var + EPS) * lg_ref[...] + lb_ref[...]
    out = _dot(_silu(yn).astype(BF16), pw_ref[...].astype(BF16))
    u_ref[...] = (out * _silu(gate_ref[...])).astype(u_ref.dtype)


def _state_spec(sl, nbb, rows):
    if sl is None:
        return pl.BlockSpec((nbb, rows, BRANCH_W), lambda b, i: (b, 0, 0))
    return pl.BlockSpec((None, nbb, rows, BRANCH_W), lambda b, i: (sl, b, 0, 0))


def _b_mixer(zm, zg, state, cw, cb, lg, lb, pw, l, sl, nbatch, t, tb, nbb, out_dtype):
    nt = t // tb
    assert nbb == 1 or nt == 1
    hist = B_CONV - 1
    rows = nbb * tb

    def vec(arr_rows):
        return pl.BlockSpec((None, arr_rows, BRANCH_W), lambda b, i: (l, 0, 0))

    return pl.pallas_call(
        functools.partial(_b_kernel, tb, nbb),
        grid=(nbatch // nbb, nt),
        in_specs=[
            _window_spec(rows, BRANCH_W, lambda b, i: ((b * nt + i) * rows, OFF_BA)),
            _window_spec(rows, BRANCH_W, lambda b, i: ((b * nt + i) * rows, OFF_BB)),
            _state_spec(sl, nbb, hist),
            vec(B_CONV), vec(1), vec(1), vec(1), vec(BRANCH_W),
            pl.BlockSpec((rows, BRANCH_W), lambda b, i: (b * nt + i, 1)),
        ],
        out_specs=[pl.BlockSpec((rows, BRANCH_W), lambda b, i: (b * nt + i, 0)),
                   pl.BlockSpec((nbb, hist, BRANCH_W), lambda b, i: (b, 0, 0))],
        out_shape=[jax.ShapeDtypeStruct((nbatch * t, BRANCH_W), out_dtype),
                   jax.ShapeDtypeStruct((nbatch, hist, BRANCH_W), F32)],
        scratch_shapes=[pltpu.VMEM((B_PAD + tb, BRANCH_W), F32), pltpu.VMEM((rows, BRANCH_W), F32),
                        pltpu.VMEM((SUBLANES - 1, tb + B_PAD - SUBLANES, BRANCH_W), F32),
                        pltpu.VMEM((B_CONV, SUBLANES, BRANCH_W), F32)],
        compiler_params=_cparams(("arbitrary", "arbitrary")),
        name="b_conv",
    )(zm, zm, state, cw, cb, lg, lb, pw, zg)


C_PAD = 16


def _c_kernel(tb, nbb, pos0, u_ref, st_ref, w_ref, sc_ref, gate_ref, o_ref, sto_ref, ubuf, s2, s4, s8, dbuf):
    i = pl.program_id(1)
    pos1 = (pos0 + i * tb + lax.broadcasted_iota(jnp.int32, (tb, 1), 0) + 1).astype(F32)
    grp = lax.broadcasted_iota(jnp.int32, (1, BRANCH_W), 1) // C_GROUP
    n2, n4, n8 = tb + 14, tb + 12, tb + 8
    for bi in range(nbb):
        seq = slice(bi * tb, (bi + 1) * tb)

        @pl.when(i == 0)
        def _():
            ubuf[1:C_PAD, :] = st_ref[bi]

        u = u_ref[seq, :]
        ubuf[C_PAD:C_PAD + tb, :] = u
        s2[0:n2, :] = ubuf[1:1 + n2, :] + ubuf[2:2 + n2, :]
        s4[0:n4, :] = s2[0:n4, :] + s2[2:2 + n4, :]
        s8[0:n8, :] = s4[0:n8, :] + s4[4:4 + n8, :]
        sums = (s2[14:14 + tb, :], s4[12:12 + tb, :], s8[8:8 + tb, :], s8[0:tb, :] + s8[8:8 + tb, :])
        mean = sums[3] / jnp.minimum(pos1, float(C_WINDOWS[3]))
        for gi in (2, 1, 0):
            mean = jnp.where(grp == gi, sums[gi] / jnp.minimum(pos1, float(C_WINDOWS[gi])), mean)
        dbuf[seq, :] = mean - u
        tail = ubuf[1 + tb:C_PAD + tb, :]
        sto_ref[bi] = tail
        ubuf[1:C_PAD, :] = tail
    y = _dot(dbuf[...].astype(BF16), w_ref[...].astype(BF16)) * sc_ref[...]
    o_ref[...] = (y * _silu(gate_ref[...])).astype(o_ref.dtype)


def _c_mixer(zm, zg, state, wbd, scale, l, sl, nbatch, t, tb, nbb, pos0, out_dtype):
    nt = t // tb
    assert nbb == 1 or nt == 1
    rows = nbb * tb
    return pl.pallas_call(
        functools.partial(_c_kernel, tb, nbb, pos0),
        grid=(nbatch // nbb, nt),
        in_specs=[
            _window_spec(rows, BRANCH_W, lambda b, i: ((b * nt + i) * rows, OFF_CU)),
            _state_spec(sl, nbb, C_STATE),
            pl.BlockSpec((None, BRANCH_W, BRANCH_W), lambda b, i: (l, 0, 0)),
            pl.BlockSpec((None, 1, BRANCH_W), lambda b, i: (l, 0, 0)),
            pl.BlockSpec((rows, BRANCH_W), lambda b, i: (b * nt + i, 2)),
        ],
        out_specs=[pl.BlockSpec((rows, BRANCH_W), lambda b, i: (b * nt + i, 0)),
                   pl.BlockSpec((nbb, C_STATE, BRANCH_W), lambda b, i: (b, 0, 0))],
        out_shape=[jax.ShapeDtypeStruct((nbatch * t, BRANCH_W), out_dtype),
                   jax.ShapeDtypeStruct((nbatch, C_STATE, BRANCH_W), F32)],
        scratch_shapes=[pltpu.VMEM((C_PAD + tb, BRANCH_W), F32)] * 4 + [pltpu.VMEM((rows, BRANCH_W), F32)],
        compiler_params=_cparams(("arbitrary", "arbitrary")),
        name="c_pool",
    )(zm, state, wbd, scale, zg)


D_COMBINE_ROWS = 512
D_BLOCK_UNROLL = (8, 8, 4)


def _d_prompt_kernel(keep, q_ref, k_ref, v_ref, gate_ref, *rest):
    u_ref, st_ref, o_s, l_s = rest[-4:]
    heads = LANES // HEAD_DIM
    for kv, ref in ((0, k_ref), (1, v_ref)):
        st_ref[kv] = ref[SEQ - keep:SEQ, :].T.reshape(heads, HEAD_DIM, keep)
    lo = _lane_lo()
    rp = lax.broadcasted_iota(jnp.int32, (2 * BLOCK, BLOCK), 0) & (BLOCK - 1)
    jp = lax.broadcasted_iota(jnp.int32, (2 * BLOCK, BLOCK), 1)
    valid_c = jp <= rp
    valid_p = jp >= rp
    for p, (_, d) in enumerate(D_PATTERNS):
        nb = SEQ // d // BLOCK
        span = BLOCK * d

        def rows(start):
            return pl.ds(start, BLOCK) if d == 1 else pl.ds(start, BLOCK, stride=d)

        def block(it, carry):
            kp, vp = carry
            r = it // nb
            i = it - r * nb
            cur = r + i * span
            prev_bias = jnp.where(i > 0, 0.0, NEG)
            qc = q_ref[rows(cur), :] * ATTN_SCALE
            qs = jnp.concatenate([jnp.where(lo, qc, 0.0), jnp.where(lo, 0.0, qc)], axis=0).astype(BF16)
            kc = k_ref[rows(cur), :].astype(BF16)
            vc = v_ref[rows(cur), :].astype(BF16)
            o, lse = _softmax_attn(qs, kp, kc, vp, vc, valid_p, valid_c, None, prev_bias)
            o_s[p, rows(cur), :] = jnp.where(lo, o[0:BLOCK], o[BLOCK:2 * BLOCK])
            l_s[p, rows(cur), :] = jnp.where(lo, lse[0:BLOCK], lse[BLOCK:2 * BLOCK])
            return kc, vc

        zero = jnp.zeros((BLOCK, LANES), BF16)
        lax.fori_loop(0, d * nb, block, (zero, zero), unroll=D_BLOCK_UNROLL[p])

    def combine(t, carry):
        sl = pl.ds(pl.multiple_of(t * D_COMBINE_ROWS, D_COMBINE_ROWS), D_COMBINE_ROWS)
        la, lb, lc = l_s[0, sl, :], l_s[1, sl, :], l_s[2, sl, :]
        m = jnp.maximum(jnp.maximum(la, lb), lc)
        ea, eb, ec = jnp.exp(la - m), jnp.exp(lb - m), jnp.exp(lc - m)
        tot = ea + eb + ec
        y = (ea / tot) * o_s[0, sl, :] + (eb / tot) * o_s[1, sl, :] + (ec / tot) * o_s[2, sl, :]
        u_ref[sl, :] = (y * _silu(gate_ref[sl, :])).astype(u_ref.dtype)
        return carry

    lax.fori_loop(0, SEQ // D_COMBINE_ROWS, combine, 0)


def _d_prompt(dqn, dkn, zm, zg, prev_state, l, depth, nbatch):
    nchunk = BRANCH_W // LANES
    heads = LANES // HEAD_DIM
    keep = min(D_WINDOW, SEQ)
    gate0 = 3 * nchunk
    qk_spec = pl.BlockSpec((SEQ, LANES), lambda b, c: (b, c))
    in_specs = [qk_spec, qk_spec,
                _window_spec(SEQ, LANES, lambda b, c: (b * SEQ, pl.multiple_of(OFF_DV + c * LANES, LANES))),
                pl.BlockSpec((SEQ, LANES), lambda b, c: (b, gate0 + c))]
    args = [dqn, dkn, zm, zg]
    aliases = {}
    if prev_state is not None:
        in_specs.append(pl.BlockSpec(memory_space=pl.ANY))
        args.append(prev_state)
        aliases = {len(args) - 1: 1}
    return pl.pallas_call(
        functools.partial(_d_prompt_kernel, keep),
        grid=(nbatch, nchunk),
        in_specs=in_specs,
        out_specs=[qk_spec,
                   pl.BlockSpec((None, None, 2, heads, HEAD_DIM, keep), lambda b, c: (l, b, 0, c, 0, 0))],
        out_shape=[jax.ShapeDtypeStruct((nbatch * SEQ, BRANCH_W), BF16),
                   jax.ShapeDtypeStruct((depth, nbatch, 2, BRANCH_W // HEAD_DIM, HEAD_DIM, keep), F32)],
        input_output_aliases=aliases,
        scratch_shapes=[pltpu.VMEM((len(D_PATTERNS), SEQ, LANES), F32)] * 2,
        compiler_params=_cparams(("parallel", "parallel")),
        name="d_prompt",
    )(*args)


D_GROUP = 4
D_GROUP_W = D_GROUP * HEAD_DIM


def _d_sample_kernel(q_ref, kn_ref, vn_ref, gate_ref, cache_ref, *rest):
    u_ref, out_ref = rest[-2], rest[-1]
    t = q_ref.shape[0]
    hist = cache_ref.shape[3]
    nrow = D_GROUP * t
    q = q_ref[...]
    kn, vn = kn_ref[...], vn_ref[...]

    row_head = lax.broadcasted_iota(jnp.int32, (nrow, D_GROUP_W), 0) // t
    lane_head = lax.broadcasted_iota(jnp.int32, (nrow, D_GROUP_W), 1) // HEAD_DIM
    own = row_head == lane_head
    qbd = jnp.where(own, jnp.concatenate([q * ATTN_SCALE] * D_GROUP, axis=0), 0.0).astype(BF16)

    k_t = cache_ref[0].reshape(D_GROUP_W, hist)
    v_t = cache_ref[1].reshape(D_GROUP_W, hist)
    knp = _pad_rows(kn, BLOCK)
    vnp = _pad_rows(vn, BLOCK)
    s_c = _dot(qbd, k_t.astype(BF16))
    s_n = _dot_nt(qbd, knp.astype(BF16))

    trow_c = lax.broadcasted_iota(jnp.int32, (nrow, hist), 0) & (t - 1)
    delta_c = hist + trow_c - lax.broadcasted_iota(jnp.int32, (nrow, hist), 1)
    trow_n = lax.broadcasted_iota(jnp.int32, (nrow, BLOCK), 0) & (t - 1)
    delta_n = trow_n - lax.broadcasted_iota(jnp.int32, (nrow, BLOCK), 1)
    ecs, ens, dens, lses = [], [], [], []
    for w, d in D_PATTERNS:
        ok_c = ((delta_c & (d - 1)) == 0) & (delta_c <= w)
        ok_n = (delta_n >= 0) & ((delta_n & (d - 1)) == 0)
        sc = jnp.where(ok_c, s_c, NEG)
        sn = jnp.where(ok_n, s_n, NEG)
        m = jnp.maximum(jnp.max(sc, axis=-1, keepdims=True), jnp.max(sn, axis=-1, keepdims=True))
        ec = jnp.exp(sc - m)
        en = jnp.exp(sn - m)
        den = jnp.sum(ec, axis=-1, keepdims=True) + jnp.sum(en, axis=-1, keepdims=True)
        ecs.append(ec)
        ens.append(en)
        dens.append(den)
        lses.append(m + jnp.log(den))
    mm = jnp.maximum(jnp.maximum(lses[0], lses[1]), lses[2])
    wts = [jnp.exp(x - mm) for x in lses]
    tot = wts[0] + wts[1] + wts[2]
    coef = [wts[p] / tot / dens[p] for p in range(3)]
    pc = coef[0] * ecs[0] + coef[1] * ecs[1] + coef[2] * ecs[2]
    pn = coef[0] * ens[0] + coef[1] * ens[1] + coef[2] * ens[2]
    yfull = _dot_nt(pc.astype(BF16), v_t.astype(BF16)) + _dot(pn.astype(BF16), vnp.astype(BF16))
    y = jnp.zeros((t, D_GROUP_W), F32)
    for j in range(D_GROUP):
        y = y + jnp.where(own[j * t:(j + 1) * t], yfull[j * t:(j + 1) * t], 0.0)
    u_ref[...] = (y * _silu(gate_ref[...])).astype(u_ref.dtype)

    lane = lax.broadcasted_iota(jnp.int32, (1, LANES), 1)
    is_new = lane >= LANES - t
    for kv, new in ((0, knp), (1, vnp)):
        new_t = pltpu.roll(new.T, LANES - t, 1)
        old = cache_ref[kv].reshape(D_GROUP_W, hist)
        shifted = pltpu.roll(old, hist - t, 1)
        out_ref[kv, :, :, 0:hist - LANES] = shifted[:, 0:hist - LANES].reshape(D_GROUP, HEAD_DIM, hist - LANES)
        tail = jnp.where(is_new, new_t, shifted[:, hist - LANES:hist])
        out_ref[kv, :, :, hist - LANES:hist] = tail.reshape(D_GROUP, HEAD_DIM, LANES)


def _d_sample(dqn, dkn, zm, zg, cache_t, prev_out, l, nbatch, t):
    ngroup = cache_t.shape[3] // D_GROUP
    gate0 = 3 * BRANCH_W // D_GROUP_W
    cache_spec = pl.BlockSpec((None, None, 2, D_GROUP, HEAD_DIM, cache_t.shape[5]),
                              lambda b, g: (l, b, 0, g, 0, 0))
    small = pl.BlockSpec((t, D_GROUP_W), lambda b, g: (b, g))
    in_specs = [small, small,
                _window_spec(t, D_GROUP_W, lambda b, g: (b * t, pl.multiple_of(OFF_DV + g * D_GROUP_W, LANES))),
                pl.BlockSpec((t, D_GROUP_W), lambda b, g: (b, gate0 + g)),
                cache_spec]
    args = [dqn, dkn, zm, zg, cache_t]
    aliases = {}
    if prev_out is not None:
        in_specs.append(pl.BlockSpec(memory_space=pl.ANY))
        args.append(prev_out)
        aliases = {len(args) - 1: 1}
    return pl.pallas_call(
        _d_sample_kernel,
        grid=(nbatch, ngroup),
        in_specs=in_specs,
        out_specs=[small, cache_spec],
        out_shape=[jax.ShapeDtypeStruct((nbatch * t, BRANCH_W), F32),
                   jax.ShapeDtypeStruct(cache_t.shape, F32)],
        input_output_aliases=aliases,
        compiler_params=_cparams(("parallel", "parallel")),
        name="d_sample",
    )(*args)


M_WIN = 2 * LANES
M_WIN_START = (0, 128, 384, 512)


def _m_head_mask(h):
    lane = lax.broadcasted_iota(jnp.int32, (1, M_WIN), 1)
    first = h * M_HEAD_DIM - M_WIN_START[h]
    return (lane >= first) & (lane < first + M_HEAD_DIM)


def _mem_kv_kernel(x_ref, g_ref, w_ref, kg_ref, ones_ref, o_ref):
    x = x_ref[...]
    ms = jnp.mean(x * x, axis=-1, keepdims=True)
    xn = (x * lax.rsqrt(ms + EPS) * g_ref[...]).astype(BF16)
    kv = _dot(xn, w_ref[...].astype(BF16))
    k = kv[:, 0:BRANCH_W]
    kms = _seg_sum(k * k, ones_ref[...]) * (1.0 / M_HEAD_DIM)
    o_ref[:, 0:BRANCH_W] = k * lax.rsqrt(kms + EPS) * kg_ref[...]
    o_ref[:, BRANCH_W:2 * BRANCH_W] = kv[:, BRANCH_W:2 * BRANCH_W]


def _mem_kv(mem, g, w, kg, ones_bd, l):
    nbatch = mem.shape[0]
    return pl.pallas_call(
        _mem_kv_kernel,
        grid=(nbatch,),
        in_specs=[
            pl.BlockSpec((None, N_MEM, D_MODEL), lambda b: (b, 0, 0)),
            pl.BlockSpec((None, 1, D_MODEL), lambda b: (l, 0, 0)),
            pl.BlockSpec((None, D_MODEL, 2 * BRANCH_W), lambda b: (l, 0, 0)),
            pl.BlockSpec((None, 1, BRANCH_W), lambda b: (l, 0, 0)),
            pl.BlockSpec((BRANCH_W, BRANCH_W), lambda b: (0, 0)),
        ],
        out_specs=pl.BlockSpec((None, N_MEM, 2 * BRANCH_W), lambda b: (b, 0, 0)),
        out_shape=jax.ShapeDtypeStruct((nbatch, N_MEM, 2 * BRANCH_W), F32),
        compiler_params=_cparams(("parallel",)),
        name="mem_kv",
    )(mem, g, w, kg, ones_bd)


def _m_query(q, qg, ones_bd):
    ms = _seg_sum(q * q, ones_bd) * (1.0 / M_HEAD_DIM)
    return (q * lax.rsqrt(ms + EPS) * qg) * (M_HEAD_DIM ** -0.5)


def _m_softmax(s):
    e = jnp.exp(s - jnp.max(s, axis=-1, keepdims=True))
    return (e / jnp.sum(e, axis=-1, keepdims=True)).astype(BF16)


def _m_assemble(outs):
    lo = _lane_lo()
    return jnp.concatenate([
        outs[0][:, 0:LANES],
        jnp.where(lo, outs[0][:, LANES:M_WIN], outs[1][:, 0:LANES]),
        outs[1][:, LANES:M_WIN],
        outs[2][:, 0:LANES],
        jnp.where(lo, outs[2][:, LANES:M_WIN], outs[3][:, 0:LANES]),
        outs[3][:, LANES:M_WIN],
    ], axis=1)


def _m_attn_kernel(q_ref, kv_ref, gate_ref, qg_ref, ones_ref, u_ref):
    qn = _m_query(q_ref[...], qg_ref[...], ones_ref[...])
    outs = []
    for h in range(M_HEADS):
        win = slice(M_WIN_START[h], M_WIN_START[h] + M_WIN)
        qh = jnp.where(_m_head_mask(h), qn[:, win], 0.0).astype(BF16)
        p = _m_softmax(_dot_nt(qh, kv_ref[:, win].astype(BF16)))
        vwin = slice(BRANCH_W + M_WIN_START[h], BRANCH_W + M_WIN_START[h] + M_WIN)
        outs.append(_dot(p, kv_ref[:, vwin].astype(BF16)))
    u_ref[...] = (_m_assemble(outs) * _silu(gate_ref[...])).astype(u_ref.dtype)


def _m_sample_kernel(nbb, t, q_ref, kv_ref, gate_ref, qg_ref, ones_ref, u_ref):
    qn = _m_query(q_ref[...], qg_ref[...], ones_ref[...])
    ys = []
    for bi in range(nbb):
        k_t = kv_ref[bi, 0].reshape(BRANCH_W, N_MEM)
        v_t = kv_ref[bi, 1].reshape(BRANCH_W, N_MEM)
        qb = qn[bi * t:(bi + 1) * t]
        outs = []
        for h in range(M_HEADS):
            win = slice(M_WIN_START[h], M_WIN_START[h] + M_WIN)
            qh = jnp.where(_m_head_mask(h), qb[:, win], 0.0).astype(BF16)
            p = _m_softmax(_dot(qh, k_t[win].astype(BF16)))
            outs.append(_dot_nt(p, v_t[win].astype(BF16)))
        ys.append(_m_assemble(outs))
    y = jnp.concatenate(ys, axis=0)
    u_ref[...] = (y * _silu(gate_ref[...])).astype(u_ref.dtype)


def _m_sample(zm, zg, cache_t, qg, ones_bd, l, nbatch, t, nbb):
    rows = nbb * t
    return pl.pallas_call(
        functools.partial(_m_sample_kernel, nbb, t),
        grid=(nbatch // nbb,),
        in_specs=[
            _window_spec(rows, BRANCH_W, lambda i: (i * rows, OFF_MQ)),
            pl.BlockSpec((None, nbb, 2, M_HEADS, M_HEAD_DIM, N_MEM), lambda i: (l, i, 0, 0, 0, 0)),
            pl.BlockSpec((rows, BRANCH_W), lambda i: (i, 4)),
            pl.BlockSpec((None, 1, BRANCH_W), lambda i: (l, 0, 0)),
            pl.BlockSpec((BRANCH_W, BRANCH_W), lambda i: (0, 0)),
        ],
        out_specs=pl.BlockSpec((rows, BRANCH_W), lambda i: (i, 0)),
        out_shape=jax.ShapeDtypeStruct((nbatch * t, BRANCH_W), F32),
        compiler_params=_cparams(("parallel",)),
        name="m_sample",
    )(zm, cache_t, zg, qg, ones_bd)


def _m_attn(zm, zg, mkv, qg, ones_bd, l, nbatch, t, tq, out_dtype):
    nt = t // tq
    kv_spec = pl.BlockSpec((None, N_MEM, 2 * BRANCH_W), lambda b, i: (b, 0, 0))
    return pl.pallas_call(
        _m_attn_kernel,
        grid=(nbatch, nt),
        in_specs=[
            _window_spec(tq, BRANCH_W, lambda b, i: ((b * nt + i) * tq, OFF_MQ)),
            kv_spec,
            pl.BlockSpec((tq, BRANCH_W), lambda b, i: (b * nt + i, 4)),
            pl.BlockSpec((None, 1, BRANCH_W), lambda b, i: (l, 0, 0)),
            pl.BlockSpec((BRANCH_W, BRANCH_W), lambda b, i: (0, 0)),
        ],
        out_specs=pl.BlockSpec((tq, BRANCH_W), lambda b, i: (b * nt + i, 0)),
        out_shape=jax.ShapeDtypeStruct((nbatch * t, BRANCH_W), out_dtype),
        compiler_params=_cparams(("parallel", "parallel")),
        name="m_attn",
    )(zm, mkv, zg, qg, ones_bd)


def _out_proj_kernel(ua, ub, uc, ud, um, w_ref, h_ref, o_ref):
    acc = h_ref[...]
    for g, u_ref in enumerate((ua, ub, uc, ud, um)):
        wg = w_ref[g * BRANCH_W:(g + 1) * BRANCH_W, :].astype(BF16)
        acc = acc + _dot(u_ref[...].astype(BF16), wg)
    o_ref[...] = acc


def _out_proj(us, w, h, l, tm, tn):
    m = h.shape[0]
    u_spec = pl.BlockSpec((tm, BRANCH_W), lambda i, j: (i, 0), pipeline_mode=pl.Buffered(1))
    return pl.pallas_call(
        _out_proj_kernel,
        grid=(m // tm, D_MODEL // tn),
        in_specs=[u_spec] * N_BRANCH + [
            pl.BlockSpec((None, MIX_W, tn), lambda i, j: (l, 0, j)),
            pl.BlockSpec((tm, tn), lambda i, j: (i, j)),
        ],
        out_specs=pl.BlockSpec((tm, tn), lambda i, j: (i, j)),
        out_shape=jax.ShapeDtypeStruct((m, D_MODEL), F32),
        compiler_params=_cparams(("parallel", "parallel")),
        name="out_proj",
    )(*us, w, h)


def _rope_tables(pos):
    half = ROT_DIM // 2
    inv = ROPE_THETA ** (-jnp.arange(half, dtype=F32) / half)
    ang = pos.astype(F32)[:, None] * inv[None, :]
    cos, sin = jnp.cos(ang), jnp.sin(ang)
    n = pos.shape[0]
    rest = HEAD_DIM - ROT_DIM
    one_head = lambda a, b, fill: jnp.concatenate([a, b, jnp.full((n, rest), fill, F32)], axis=1)
    zero = jnp.zeros_like(sin)
    cos_t = one_head(cos, cos, 1.0)
    slo_t = one_head(-sin, zero, 0.0)
    shi_t = one_head(zero, sin, 0.0)
    return tuple(jnp.concatenate([a, a], axis=1) for a in (cos_t, slo_t, shi_t))


def _block_ones(width, seg):
    idx = jnp.arange(width) // seg
    return (idx[:, None] == idx[None, :]).astype(BF16)


def _tile_lanes(x, width):
    return jnp.tile(x, (1, width // x.shape[-1]))


def kernel(x_prompt, x_sample, cache_a_kv, state_b_conv, state_c_pool, cache_d_kv, cache_mem_kv, mem_prompt,
           norm_g, w_in, a_q_norm, a_k_norm, a_sinks, b_conv_w, b_conv_b, b_ln_g, b_ln_b, b_w_pw, c_w_group,
           c_scale, d_q_norm, d_k_norm, m_norm_g, m_w_kv, m_q_norm, m_k_norm, w_out):
    bp, seq = x_prompt.shape[0], x_prompt.shape[1]
    bs, ts = x_sample.shape[0], x_sample.shape[1]
    depth = w_in.shape[0]
    hist_d = cache_d_kv.shape[2]

    hp = x_prompt.reshape(bp * seq, D_MODEL)
    hs = x_sample.reshape(bs * ts, D_MODEL)

    tabs_p = _rope_tables(jnp.arange(seq, dtype=jnp.int32))
    tabs_s = tuple(jnp.tile(a, (bs, 1)) for a in _rope_tables(PAST_LEN + jnp.arange(ts, dtype=jnp.int32)))
    ones64 = _block_ones(LANES, HEAD_DIM)
    ones192 = _block_ones(BRANCH_W, M_HEAD_DIM)

    row = lambda a: a.reshape(depth, 1, a.shape[-1])
    norm_g3, m_norm_g3 = row(norm_g), row(m_norm_g)
    qk_gains = jnp.stack([_tile_lanes(g, LANES) for g in (a_q_norm, a_k_norm, d_q_norm, d_k_norm)], axis=1)
    m_qg, m_kg = row(_tile_lanes(m_q_norm, BRANCH_W)), row(_tile_lanes(m_k_norm, BRANCH_W))
    b_cb, b_lg, b_lb, c_sc = row(b_conv_b), row(b_ln_g), row(b_ln_b), row(c_scale)
    c_wbd = jnp.zeros((depth, BRANCH_W, BRANCH_W), F32)
    for gi in range(len(C_WINDOWS)):
        sl = slice(gi * C_GROUP, (gi + 1) * C_GROUP)
        c_wbd = c_wbd.at[:, sl, sl].set(c_w_group[:, gi])

    cache_a = cache_a_kv.reshape(depth, bs, A_WINDOW, 2 * A_KV_W)
    cache_d_t = jnp.transpose(cache_d_kv, (0, 1, 3, 4, 5, 2))
    sd_t = None
    cache_m_t = jnp.transpose(cache_mem_kv, (0, 1, 3, 4, 5, 2))
    zero_b = jnp.zeros((bp, B_CONV - 1, BRANCH_W), F32)
    zero_c = jnp.zeros((bp, C_STATE, BRANCH_W), F32)

    pa, pb, pc, pm, sa, sb, sc = ([] for _ in range(7))
    pd_t = None
    for l in range(depth):
        mkv = _mem_kv(mem_prompt, m_norm_g3, m_w_kv, m_kg, ones192, l)
        pm.append(mkv)

        xn = _rms_norm(hp, norm_g3, l, 512)
        zm = _in_proj(xn, w_in, l, 0, MIXIN_W, 512, PROJ_ROWS)
        zg = _in_proj(xn, w_in, l, OFF_GATE, MIX_W, 768, PROJ_ROWS)
        aqn, akn, dqn, dkn = _prep(zm, tabs_p, qk_gains, ones64, l, 512)
        ua, a_st = _a_prompt(aqn, akn, zm, zg, a_sinks, l, bp)
        ub, b_st = _b_mixer(zm, zg, zero_b, b_conv_w, b_cb, b_lg, b_lb, b_w_pw, l, None, bp, seq, 512, 1, BF16)
        uc, c_st = _c_mixer(zm, zg, zero_c, c_wbd, c_sc, l, None, bp, seq, 512, 1, 0, BF16)
        ud, pd_t = _d_prompt(dqn, dkn, zm, zg, pd_t, l, depth, bp)
        um = _m_attn(zm, zg, mkv, m_qg, ones192, l, bp, seq, 512, BF16)
        hp = _out_proj((ua, ub, uc, ud, um), w_out, hp, l, PROJ_ROWS, 256)
        pa.append(a_st)
        pb.append(b_st)
        pc.append(c_st)

        rows_s = bs * ts
        xn = _rms_norm(hs, norm_g3, l, rows_s)
        zm = _in_proj(xn, w_in, l, 0, MIXIN_W, 512, rows_s)
        zg = _in_proj(xn, w_in, l, OFF_GATE, MIX_W, 768, rows_s)
        aqn, akn, dqn, dkn = _prep(zm, tabs_s, qk_gains, ones64, l, rows_s)
        ua, a_st = _a_sample(aqn, akn, zm, zg, a_sinks, cache_a, l, bs, ts, SAMPLE_GROUP)
        ub, b_st = _b_mixer(zm, zg, state_b_conv, b_conv_w, b_cb, b_lg, b_lb, b_w_pw, l, l, bs, ts, ts,
                            SAMPLE_GROUP, F32)
        uc, c_st = _c_mixer(zm, zg, state_c_pool, c_wbd, c_sc, l, l, bs, ts, ts, SAMPLE_GROUP, PAST_LEN, F32)
        ud, sd_t = _d_sample(dqn, dkn, zm, zg, cache_d_t, sd_t, l, bs, ts)
        um = _m_sample(zm, zg, cache_m_t, m_qg, ones192, l, bs, ts, SAMPLE_GROUP)
        hs = _out_proj((ua, ub, uc, ud, um), w_out, hs, l, rows_s, 512)
        sa.append(a_st)
        sb.append(b_st)
        sc.append(c_st)

    kv5 = lambda a, heads, hd: a.reshape(a.shape[:-1] + (2, heads, hd))
    s_d_kv = jnp.transpose(sd_t, (0, 1, 5, 2, 3, 4))
    p_d_kv = jnp.transpose(pd_t, (0, 1, 5, 2, 3, 4))
    return (hp.reshape(bp, seq, D_MODEL), hs.reshape(bs, ts, D_MODEL),
            kv5(jnp.stack(pa), A_KV_HEADS, HEAD_DIM), jnp.stack(pb), jnp.stack(pc),
            p_d_kv, kv5(jnp.stack(pm), M_HEADS, M_HEAD_DIM),
            kv5(jnp.stack(sa), A_KV_HEADS, HEAD_DIM), jnp.stack(sb), jnp.stack(sc), s_d_kv)
```
